```python
import jax, jax.numpy as jnp
from jax import lax
import numpy as np

D_MODEL = 2048
BATCH = 4
SEQ = 4096
DEPTH = 1

HEAD_DIM = 64
FOX_HEADS = 16
RWKV_HEADS = 16
D_FOX = FOX_HEADS * HEAD_DIM
D_RWKV = RWKV_HEADS * HEAD_DIM
D_MIX = D_FOX + D_RWKV
DECAY_LORA = 96
ICLR_LORA = 96
GATE_LORA = 256
D_FF = 4 * D_MODEL
BLOCK_Q = 128
N_FOX_IN = 3 * D_FOX + FOX_HEADS
N_RWKV_IN = 3 * D_RWKV + DECAY_LORA + ICLR_LORA + GATE_LORA
N_IN = N_FOX_IN + N_RWKV_IN
NORM_EPS = 1e-6
LNX_EPS = 64e-5
NEG_INF = -1e30

kernel_name = 'hybrid_fox_rwkv7_adaln_layer'


def rms_norm(x, g, eps=NORM_EPS):
    xf = x.astype(jnp.float32)
    y = xf * lax.rsqrt(jnp.mean(xf * xf, axis=-1, keepdims=True) + eps)
    return (y * g.astype(jnp.float32)).astype(x.dtype)


def ada_modulate(h, shift, scale):
    return h * (1 + scale[:, None, :]) + shift[:, None, :]


def forgetting_attention(q, k, v, log_f):
    B, T, H, dh = q.shape
    nb = T // BLOCK_Q
    dcum = jnp.cumsum(log_f, axis=1)
    qb = q.reshape(B, nb, BLOCK_Q, H, dh).transpose(1, 0, 3, 2, 4)
    dqb = dcum.reshape(B, nb, BLOCK_Q, H).transpose(1, 0, 3, 2)
    dk = dcum.transpose(0, 2, 1)
    kpos = jnp.arange(T)
    scale = HEAD_DIM ** -0.5

    def one_block(args):
        q_blk, dq, i = args
        qpos = i * BLOCK_Q + jnp.arange(BLOCK_Q)
        s = jnp.einsum('bhqd,bkhd->bhqk', q_blk, k).astype(jnp.float32) * scale
        s = s + dq[..., None] - dk[:, :, None, :]
        s = jnp.where(kpos[None, :] <= qpos[:, None], s, NEG_INF)
        p = jax.nn.softmax(s, axis=-1)
        return jnp.einsum('bhqk,bkhd->bqhd', p.astype(v.dtype), v)

    out = lax.map(one_block, (qb, dqb, jnp.arange(nb)))
    return out.transpose(1, 0, 2, 3, 4).reshape(B, T, H, dh)


def rwkv7_scan(r, decay, k, v, kk, a):
    B, T, H, N = r.shape

    def step(S, inp):
        r_t, d_t, k_t, v_t, kk_t, a_t = inp
        sa = jnp.einsum('bhvk,bhk->bhv', S, -kk_t)
        S = S * d_t[:, :, None, :] + sa[..., None] * (kk_t * a_t)[:, :, None, :] + v_t[..., None] * k_t[:, :, None, :]
        y = jnp.einsum('bhvk,bhk->bhv', S, r_t)
        return S, y

    xs = tuple(jnp.moveaxis(t, 1, 0) for t in (r, decay, k, v, kk, a))
    s0 = jnp.zeros((B, H, N, N), jnp.float32)
    _, y = lax.scan(step, s0, xs)
    return jnp.moveaxis(y, 0, 1)


def rwkv7_time_mix(z, w0, decay_b, a0, a_b, g_b, k_k, k_a, r_k, lnx_g, lnx_b):
    B, T, _ = z.shape
    zf = z.astype(jnp.float32)
    o1 = D_RWKV
    o2 = 2 * D_RWKV
    o3 = 3 * D_RWKV
    o4 = o3 + DECAY_LORA
    o5 = o4 + ICLR_LORA
    r, k, v, xw, xa, xg = jnp.split(zf, [o1, o2, o3, o4, o5], axis=-1)
    w = -jax.nn.softplus(-(w0 + jnp.tanh(xw) @ decay_b)) - 0.5
    decay = jnp.exp(-jnp.exp(w))
    a = jax.nn.sigmoid(a0 + xa @ a_b)
    g = jax.nn.sigmoid(xg) @ g_b
    heads = lambda t: t.reshape(B, T, RWKV_HEADS, HEAD_DIM)
    kk = heads(k * k_k)
    kk = kk * lax.rsqrt(jnp.maximum(jnp.sum(kk * kk, axis=-1, keepdims=True), 1e-24))
    k = k * (1 + (a - 1) * k_a)
    r, k, v, decay, a = heads(r), heads(k), heads(v), heads(decay), heads(a)
    y = rwkv7_scan(r, decay, k, v, kk, a)
    mu = jnp.mean(y, axis=-1, keepdims=True)
    var = jnp.mean(jnp.square(y - mu), axis=-1, keepdims=True)
    y = (y - mu) * lax.rsqrt(var + LNX_EPS) * lnx_g.reshape(RWKV_HEADS, HEAD_DIM) + lnx_b.reshape(RWKV_HEADS, HEAD_DIM)
    y = y + jnp.sum(r * k * r_k, axis=-1, keepdims=True) * v
    return y.reshape(B, T, D_RWKV) * g


def setup_inputs(seed: int = 0) -> dict:
    key = jax.random.key(seed)
    ks = jax.random.split(key, 26)
    f32 = jnp.float32
    nrm = lambda kk, shape, s: jax.random.normal(kk, shape, f32) * s
    L = DEPTH
    return {
        'x': nrm(ks[0], (BATCH, SEQ, D_MODEL), 1.0),
        'c': nrm(ks[1], (BATCH, D_MODEL), 1.0),
        'w_ada': nrm(ks[2], (L, D_MODEL, 6 * D_MODEL), 0.5 * D_MODEL ** -0.5),
        'b_ada': nrm(ks[3], (L, 6 * D_MODEL), 0.02),
        'norm1_g': 1.0 + nrm(ks[4], (L, D_MODEL), 0.02),
        'w_in': nrm(ks[5], (L, D_MODEL, N_IN), D_MODEL ** -0.5),
        'b_forget': 2.0 + nrm(ks[6], (L, FOX_HEADS), 0.5),
        'q_norm_g': 1.0 + nrm(ks[7], (L, HEAD_DIM), 0.02),
        'k_norm_g': 1.0 + nrm(ks[8], (L, HEAD_DIM), 0.02),
        'fox_out_g': 1.0 + nrm(ks[9], (L, D_FOX), 0.02),
        'shift_mu': jax.random.uniform(ks[10], (L, N_RWKV_IN), f32, 0.0, 1.0),
        'w0': jax.random.uniform(ks[11], (L, D_RWKV), f32, -6.0, 0.0),
        'decay_b': nrm(ks[12], (L, DECAY_LORA, D_RWKV), 0.5 * DECAY_LORA ** -0.5),
        'a0': nrm(ks[13], (L, D_RWKV), 0.5),
        'a_b': nrm(ks[14], (L, ICLR_LORA, D_RWKV), 0.5 * ICLR_LORA ** -0.5),
        'g_b': nrm(ks[15], (L, GATE_LORA, D_RWKV), GATE_LORA ** -0.5),
        'k_k': 0.85 + nrm(ks[16], (L, D_RWKV), 0.02),
        'k_a': 1.0 + nrm(ks[17], (L, D_RWKV), 0.02),
        'r_k': nrm(ks[18], (L, RWKV_HEADS, HEAD_DIM), 0.1),
        'lnx_g': 1.0 + nrm(ks[19], (L, D_RWKV), 0.02),
        'lnx_b': nrm(ks[20], (L, D_RWKV), 0.02),
        'w_out': nrm(ks[21], (L, D_MIX, D_MODEL), D_MIX ** -0.5),
        'norm2_g': 1.0 + nrm(ks[22], (L, D_MODEL), 0.02),
        'w_mlp_up': nrm(ks[23], (L, D_MODEL, D_FF), D_MODEL ** -0.5),
        'w_mlp_down': nrm(ks[24], (L, D_FF, D_MODEL), D_FF ** -0.5),
    }


def reference(x, c, w_ada, b_ada, norm1_g, w_in, b_forget, q_norm_g, k_norm_g, fox_out_g, shift_mu, w0, decay_b, a0, a_b, g_b, k_k, k_a, r_k, lnx_g, lnx_b, w_out, norm2_g, w_mlp_up, w_mlp_down):
    B, T, _ = x.shape
    cond = jax.nn.silu(c)
    for l in range(DEPTH):
        mod = cond @ w_ada[l] + b_ada[l]
        sh1, sc1, gt1, sh2, sc2, gt2 = jnp.split(mod, 6, axis=-1)

        h = ada_modulate(rms_norm(x, norm1_g[l]), sh1, sc1)
        proj = h @ w_in[l]
        fox_cols = proj[..., :N_FOX_IN]
        rwkv_cols = proj[..., N_FOX_IN:]

        q, k, v, f = jnp.split(fox_cols, [D_FOX, 2 * D_FOX, 3 * D_FOX], axis=-1)
        q = rms_norm(q.reshape(B, T, FOX_HEADS, HEAD_DIM), q_norm_g[l])
        k = rms_norm(k.reshape(B, T, FOX_HEADS, HEAD_DIM), k_norm_g[l])
        v = v.reshape(B, T, FOX_HEADS, HEAD_DIM)
        log_f = jax.nn.log_sigmoid((f + b_forget[l]).astype(jnp.float32))
        fox = forgetting_attention(q, k, v, log_f)
        fox = rms_norm(fox, fox_out_g[l].reshape(FOX_HEADS, HEAD_DIM)).reshape(B, T, D_FOX)

        rwkv_prev = jnp.pad(rwkv_cols, ((0, 0), (1, 0), (0, 0)))[:, :T]
        rwkv_in = rwkv_cols + (rwkv_prev - rwkv_cols) * shift_mu[l]
        rwkv = rwkv7_time_mix(rwkv_in, w0[l], decay_b[l], a0[l], a_b[l], g_b[l], k_k[l], k_a[l], r_k[l], lnx_g[l], lnx_b[l])

        mix = jnp.concatenate([fox.astype(x.dtype), rwkv.astype(x.dtype)], axis=-1)
        x = x + gt1[:, None, :] * (mix @ w_out[l])

        h = ada_modulate(rms_norm(x, norm2_g[l]), sh2, sc2)
        ff = jnp.square(jax.nn.relu(h @ w_mlp_up[l])) @ w_mlp_down[l]
        x = x + gt2[:, None, :] * ff
    return x
```

```python
import functools

import jax
import jax.numpy as jnp
from jax import lax
from jax.experimental import pallas as pl
from jax.experimental.pallas import tpu as pltpu

_F32 = jnp.float32
_BF16 = jnp.bfloat16

HEAD_DIM = 64
PAIR = 2 * HEAD_DIM
N_HEADS = 16
D_HEADS = N_HEADS * HEAD_DIM
N_PAIRS = N_HEADS // 2
DECAY_LORA = 96
ICLR_LORA = 96
GATE_LORA = 256
SMALL_W = 512
F_LANE0 = DECAY_LORA
NORM_EPS = 1e-6
LNX_EPS = 64e-5
NEG_INF = -1e30
CHUNK_LOG2 = 6
CHUNK = 1 << CHUNK_LOG2
MIB = 1024 * 1024


def _cparams(sem, vmem_mib):
    return pltpu.CompilerParams(dimension_semantics=sem, vmem_limit_bytes=vmem_mib * MIB)


def _dot(a, b):
    return jnp.dot(a.astype(_BF16), b.astype(_BF16), preferred_element_type=_F32)


def _dot_nt(a, b):
    return lax.dot_general(a.astype(_BF16), b.astype(_BF16), (((1,), (1,)), ((), ())),
                           preferred_element_type=_F32)


def _dot_tn(a, b):
    return lax.dot_general(a.astype(_BF16), b.astype(_BF16), (((0,), (0,)), ((), ())),
                           preferred_element_type=_F32)


def _split3(x):
    h = x.astype(_BF16).astype(_F32)
    r = x - h
    m = r.astype(_BF16).astype(_F32)
    l = (r - m).astype(_BF16).astype(_F32)
    return h, m, l


def _dot_exact_rhs(a_bf16, x):
    return sum(jnp.dot(a_bf16, p.astype(_BF16), preferred_element_type=_F32) for p in _split3(x))


def _sigmoid(x):
    return 1.0 / (1.0 + jnp.exp(-x))


def _softplus(x):
    return jnp.maximum(x, 0.0) + jnp.log(1.0 + jnp.exp(-jnp.abs(x)))


def _half_sum(x, lo_half):
    s_lo = jnp.sum(jnp.where(lo_half, x, 0.0), axis=-1, keepdims=True)
    s_hi = jnp.sum(jnp.where(lo_half, 0.0, x), axis=-1, keepdims=True)
    return jnp.where(lo_half, s_lo, s_hi)


def _lo_half_mask():
    return lax.broadcasted_iota(jnp.int32, (1, PAIR), 1) < HEAD_DIM


def _ada_kernel(c_ref, w_ref, b_ref, o_ref):
    c = c_ref[...]
    cond = c * _sigmoid(c)
    o_ref[...] = _dot(cond, w_ref[...]) + b_ref[...]


def _ada(c, w_ada, b_ada):
    b, d = c.shape
    n = w_ada.shape[1]
    rows = 8 * pl.cdiv(b, 8)
    tn = 1024
    cp = jnp.pad(c, ((0, rows - b), (0, 0)))
    out = pl.pallas_call(
        _ada_kernel,
        grid=(n // tn,),
        in_specs=[pl.BlockSpec((rows, d), lambda j: (0, 0)),
                  pl.BlockSpec((d, tn), lambda j: (0, j)),
                  pl.BlockSpec((1, tn), lambda j: (0, j))],
        out_specs=pl.BlockSpec((rows, tn), lambda j: (0, j)),
        out_shape=jax.ShapeDtypeStruct((rows, n), _F32),
        compiler_params=_cparams(("parallel",), 40),
        name="ada",
    )(cp, w_ada, b_ada.reshape(1, n))
    return out[:b]


def _proj_kernel(x_ref, g_ref, sh_ref, sc_ref, w_ref, o_ref, h_ref):
    @pl.when(pl.program_id(1) == 0)
    def _():
        x = x_ref[...]
        y = x * lax.rsqrt(jnp.mean(x * x, axis=-1, keepdims=True) + NORM_EPS) * g_ref[...]
        h_ref[...] = (y * (1.0 + sc_ref[0]) + sh_ref[0]).astype(_BF16)

    o_ref[...] = jnp.dot(h_ref[...], w_ref[...], preferred_element_type=_F32)


def _proj(x2, g, sh, sc, w, seq):
    m, d = x2.shape
    n = w.shape[1]
    tm, tn = 512, 512
    per_b = seq // tm
    return pl.pallas_call(
        _proj_kernel,
        grid=(m // tm, n // tn),
        in_specs=[pl.BlockSpec((tm, d), lambda i, j: (i, 0)),
                  pl.BlockSpec((1, d), lambda i, j: (0, 0)),
                  pl.BlockSpec((1, 1, d), lambda i, j: (i // per_b, 0, 0)),
                  pl.BlockSpec((1, 1, d), lambda i, j: (i // per_b, 0, 0)),
                  pl.BlockSpec((d, tn), lambda i, j: (0, j))],
        out_specs=pl.BlockSpec((tm, tn), lambda i, j: (i, j)),
        out_shape=jax.ShapeDtypeStruct((m, n), _F32),
        scratch_shapes=[pltpu.VMEM((tm, d), _BF16)],
        compiler_params=_cparams(("parallel", "arbitrary"), 40),
        name="proj",
    )(x2, g, sh, sc, w)


def _fox_prep_kernel(q_ref, k_ref, v_ref, s_ref, qg_ref, kg_ref, bf_ref, qa_ref, ka_ref, va_ref,
                     carry_ref, *, tt):
    @pl.when(pl.program_id(1) == 0)
    def _():
        carry_ref[...] = jnp.zeros_like(carry_ref)

    lane = lax.broadcasted_iota(jnp.int32, (1, PAIR), 1)
    lo_half = lane < HEAD_DIM

    z = s_ref[:, 0:PAIR] + bf_ref[...]
    logf = -_softplus(-z)
    row = lax.broadcasted_iota(jnp.int32, (tt, tt), 0)
    col = lax.broadcasted_iota(jnp.int32, (tt, tt), 1)
    tri = jnp.where(row >= col, 1.0, 0.0).astype(_BF16)
    dcum = carry_ref[0:1, :] + _dot_exact_rhs(tri, logf)
    carry_ref[0:1, :] = dcum[tt - 1:tt, :]

    for p in range(N_PAIRS):
        cols = slice(p * PAIR, (p + 1) * PAIR)
        q2 = q_ref[:, cols]
        k2 = k_ref[:, cols]
        v2 = v_ref[:, cols]
        qn = q2 * lax.rsqrt(_half_sum(q2 * q2, lo_half) * (1.0 / HEAD_DIM) + NORM_EPS) * qg_ref[...]
        qn = qn * (HEAD_DIM ** -0.5)
        kn = k2 * lax.rsqrt(_half_sum(k2 * k2, lo_half) * (1.0 / HEAD_DIM) + NORM_EPS) * kg_ref[...]
        for hh in range(2):
            h = 2 * p + hh
            d1, d2, d3 = _split3(dcum[:, F_LANE0 + h:F_LANE0 + h + 1])
            base = HEAD_DIM if hh == 0 else 0
            keep = lo_half if hh == 0 else jnp.logical_not(lo_half)
            aq = jnp.where(lane < base + 3, 1.0,
                           jnp.where(lane == base + 3, d1,
                                     jnp.where(lane == base + 4, d2,
                                               jnp.where(lane == base + 5, d3, 0.0))))
            ak = jnp.where(lane == base, -d1,
                           jnp.where(lane == base + 1, -d2,
                                     jnp.where(lane == base + 2, -d3,
                                               jnp.where(lane < base + 6, 1.0, 0.0))))
            av = jnp.where(lane == base, 1.0, 0.0)
            qa_ref[0, h] = jnp.where(keep, qn, aq).astype(_BF16)
            ka_ref[0, h] = jnp.where(keep, kn, ak).astype(_BF16)
            va_ref[0, h] = jnp.where(keep, v2, av).astype(_BF16)


def _fox_prep(pmat, qg2, kg2, bf_slot, batch, seq):
    tt = 256
    nt = seq // tt
    row_map = lambda c: (lambda b, t: (b * nt + t, c))
    out_sd = jax.ShapeDtypeStruct((batch, N_HEADS, seq, PAIR), _BF16)
    out_spec = pl.BlockSpec((1, N_HEADS, tt, PAIR), lambda b, t: (b, 0, t, 0))
    vec = pl.BlockSpec((1, PAIR), lambda b, t: (0, 0))
    return pl.pallas_call(
        functools.partial(_fox_prep_kernel, tt=tt),
        grid=(batch, nt),
        in_specs=[pl.BlockSpec((tt, D_HEADS), row_map(0)),
                  pl.BlockSpec((tt, D_HEADS), row_map(1)),
                  pl.BlockSpec((tt, D_HEADS), row_map(2)),
                  pl.BlockSpec((tt, SMALL_W), row_map(6 * D_HEADS // SMALL_W)),
                  vec, vec, vec],
        out_specs=[out_spec, out_spec, out_spec],
        out_shape=[out_sd, out_sd, out_sd],
        scratch_shapes=[pltpu.VMEM((8, PAIR), _F32)],
        compiler_params=_cparams(("parallel", "arbitrary"), 40),
        name="fox_prep",
    )(pmat, pmat, pmat, pmat, qg2, kg2, bf_slot)


def _fox_attn_kernel(q_ref, k_ref, v_ref, g_ref, o_ref, m_ref, acc_ref, *, tq):
    qi = pl.program_id(2)
    lo_half = _lo_half_mask()
    row = lax.broadcasted_iota(jnp.int32, (tq, tq), 0)
    col = lax.broadcasted_iota(jnp.int32, (tq, tq), 1)
    heads = []
    for hh in range(2):
        q = q_ref[0, hh]
        m_ref[...] = jnp.full(m_ref.shape, NEG_INF, _F32)
        acc_ref[...] = jnp.zeros(acc_ref.shape, _F32)

        def step(off, causal, hh=hh, q=q):
            k = k_ref[0, hh, pl.ds(off, tq), :]
            v = v_ref[0, hh, pl.ds(off, tq), :]
            s = lax.dot_general(q, k, (((1,), (1,)), ((), ())), preferred_element_type=_F32)
            if causal:
                s = jnp.where(col <= row, s, NEG_INF)
            m_old = m_ref[...]
            m_new = jnp.maximum(m_old, jnp.max(s, axis=-1, keepdims=True))
            p = jnp.exp(s - m_new)
            acc_ref[...] = jnp.exp(m_old - m_new) * acc_ref[...] + jnp.dot(
                p.astype(_BF16), v, preferred_element_type=_F32)
            m_ref[...] = m_new

        def body(ki, carry):
            step(pl.multiple_of(ki * tq, tq), False)
            return carry

        lax.fori_loop(0, qi, body, 0)
        step(pl.multiple_of(qi * tq, tq), True)
        acc = acc_ref[...]
        denom = acc[:, HEAD_DIM:HEAD_DIM + 1] if hh == 0 else acc[:, 0:1]
        heads.append(acc / denom)
    o = jnp.where(lo_half, heads[0], heads[1])
    ms = _half_sum(o * o, lo_half) * (1.0 / HEAD_DIM)
    o_ref[0] = (o * lax.rsqrt(ms + NORM_EPS) * g_ref[...]).astype(_BF16)


def _fox_attn(qa, ka, va, og, batch, seq):
    tq = 512
    nq = seq // tq
    kv_spec = pl.BlockSpec((1, 2, seq, PAIR), lambda b, p, i: (b, p, 0, 0))
    return pl.pallas_call(
        functools.partial(_fox_attn_kernel, tq=tq),
        grid=(batch, N_PAIRS, nq),
        in_specs=[pl.BlockSpec((1, 2, tq, PAIR), lambda b, p, i: (b, p, i, 0)),
                  kv_spec, kv_spec,
                  pl.BlockSpec((1, PAIR), lambda b, p, i: (0, p))],
        out_specs=pl.BlockSpec((1, tq, PAIR), lambda b, p, i: (b, i, p)),
        out_shape=jax.ShapeDtypeStruct((batch, seq, D_HEADS), _BF16),
        scratch_shapes=[pltpu.VMEM((tq, 1), _F32), pltpu.VMEM((tq, PAIR), _F32)],
        compiler_params=_cparams(("parallel", "parallel", "arbitrary"), 40),
        name="fox_attn",
    )(qa, ka, va, og)


def _stack_heads(x, lo_half):
    return jnp.concatenate([jnp.where(lo_half, x, 0.0), jnp.where(lo_half, 0.0, x)], axis=0)


def _rwkv_kernel(r_ref, k_ref, v_ref, s_ref, mur_ref, muk_ref, muv_ref, mus_ref, w0_ref, a0_ref,
                 kk_ref, ka_ref, rk_ref, lg_ref, lb_ref, db_ref, ab_ref, gb_ref, o_ref,
                 pr_ref, pk_ref, pv_ref, ps_ref, z_ref, *, tt):
    nc = tt // CHUNK
    c2 = 2 * CHUNK

    @pl.when(pl.program_id(2) == 0)
    def _():
        pr_ref[...] = jnp.zeros_like(pr_ref)
        pk_ref[...] = jnp.zeros_like(pk_ref)
        pv_ref[...] = jnp.zeros_like(pv_ref)
        ps_ref[...] = jnp.zeros_like(ps_ref)
        z_ref[...] = jnp.zeros_like(z_ref)

    lo_half = _lo_half_mask()
    first_row = lax.broadcasted_iota(jnp.int32, (tt, 1), 0) == 0

    def shift(x_ref, prev_ref, mu_ref):
        x = x_ref[...]
        prev = jnp.where(first_row, prev_ref[0:1, :], pltpu.roll(x, 1, axis=0))
        prev_ref[0:1, :] = x[tt - 1:tt, :]
        return x + (prev - x) * mu_ref[...]

    r = shift(r_ref, pr_ref, mur_ref)
    k = shift(k_ref, pk_ref, muk_ref)
    v = shift(v_ref, pv_ref, muv_ref)
    xs = shift(s_ref, ps_ref, mus_ref)
    xw, xa, xg = xs[:, 0:PAIR], xs[:, PAIR:2 * PAIR], xs[:, 2 * PAIR:]

    w = -_softplus(-(w0_ref[...] + _dot(jnp.tanh(xw), db_ref[...]))) - 0.5
    logd = -jnp.exp(w)
    a = _sigmoid(a0_ref[...] + _dot(xa, ab_ref[...]))
    g = _dot(_sigmoid(xg), gb_ref[...])
    kk = k * kk_ref[...]
    kk = kk * lax.rsqrt(jnp.maximum(_half_sum(kk * kk, lo_half), 1e-24))
    k = k * (1.0 + (a - 1.0) * ka_ref[...])
    bb = kk * a

    row = lax.broadcasted_iota(jnp.int32, (tt, tt), 0)
    col = lax.broadcasted_iota(jnp.int32, (tt, tt), 1)
    tri = jnp.where((row >= col) & ((row >> CHUNK_LOG2) == (col >> CHUNK_LOG2)), 1.0, 0.0).astype(_BF16)
    lcum = _dot_exact_rhs(tri, logd)

    r2 = lax.broadcasted_iota(jnp.int32, (c2, c2), 0)
    q2 = lax.broadcasted_iota(jnp.int32, (c2, c2), 1)
    same = (r2 >> CHUNK_LOG2) == (q2 >> CHUNK_LOG2)
    strict = same & (r2 > q2)
    incl = same & (r2 >= q2)
    eye = jnp.where(r2 == q2, 1.0, 0.0)
    zeros_cc = jnp.zeros((c2, PAIR), _F32)

    rz, mpart, y0, z0, lc_rows = [], [], [], [], []
    for c in range(nc):
        rows = slice(c * CHUNK, (c + 1) * CHUNK)
        lc = lcum[rows]
        ld = logd[rows]
        l_end = lc[CHUNK - 1:CHUNK, :]
        lc_rows.append(l_end)
        e_pos, e_neg, e_end = jnp.exp(lc), jnp.exp(-lc), jnp.exp(l_end - lc)
        rp = _stack_heads(r[rows] * e_pos, lo_half)
        ap = _stack_heads(-kk[rows] * jnp.exp(lc - ld), lo_half)
        kq = _stack_heads(k[rows] * e_neg, lo_half)
        bq = _stack_heads(bb[rows] * e_neg, lo_half)
        kc = _stack_heads(k[rows] * e_end, lo_half)
        bc = _stack_heads(bb[rows] * e_end, lo_half)
        vs = _stack_heads(v[rows], lo_half)

        s = _dot_nt(jnp.concatenate([ap, rp], axis=0), jnp.concatenate([bq, kq], axis=0))
        a_ab = jnp.where(strict, s[:c2, :c2], 0.0)
        a_ak = jnp.where(strict, s[:c2, c2:], 0.0)
        a_rb = jnp.where(incl, s[c2:, :c2], 0.0)
        a_rk = jnp.where(incl, s[c2:, c2:], 0.0)

        tinv = eye + a_ab
        pw = a_ab
        for _ in range(5):
            pw = _dot(pw, pw)
            tinv = tinv + _dot(pw, tinv)

        wu = _dot(tinv, jnp.concatenate([ap, _dot(a_ak, vs)], axis=1))
        x = jnp.concatenate([wu, jnp.concatenate([zeros_cc, vs], axis=1)], axis=0)
        ry = _dot(jnp.concatenate([a_rb, a_rk], axis=1), x)
        mz = _dot_tn(jnp.concatenate([bc, kc], axis=0), x)
        rz.append(rp + ry[:, :PAIR])
        y0.append(ry[:, PAIR:])
        mpart.append(mz[:, :PAIR])
        z0.append(mz[:, PAIR:])

    lc_mat = jnp.concatenate(lc_rows + [jnp.zeros((PAIR - nc, PAIR), _F32)], axis=0)
    lc_t = lc_mat.T

    z = z_ref[...]
    ys = []
    for c in range(nc):
        res = _dot(jnp.concatenate([rz[c], mpart[c]], axis=0), z)
        ysh = res[:c2] + y0[c]
        ys.append(ysh[:CHUNK] + ysh[CHUNK:])
        z = jnp.exp(lc_t[:, c:c + 1]) * z + res[c2:] + z0[c]
    z_ref[...] = z
    y = jnp.concatenate(ys, axis=0)

    mu = _half_sum(y, lo_half) * (1.0 / HEAD_DIM)
    dlt = y - mu
    var = _half_sum(dlt * dlt, lo_half) * (1.0 / HEAD_DIM)
    yn = dlt * lax.rsqrt(var + LNX_EPS) * lg_ref[...] + lb_ref[...]
    bonus = _half_sum(r * k * rk_ref[...], lo_half)
    o_ref[0] = ((yn + bonus * v) * g).astype(_BF16)


def _rwkv(pmat, prm, batch, seq):
    tt = 256
    nt = seq // tt
    blk = lambda c0: pl.BlockSpec((tt, PAIR), lambda b, p, t, c0=c0: (b * nt + t, c0 + p))
    vec = pl.BlockSpec((1, PAIR), lambda b, p, t: (0, p))
    lora = lambda rows: pl.BlockSpec((rows, PAIR), lambda b, p, t: (0, p))
    c_r = 3 * D_HEADS // PAIR
    return pl.pallas_call(
        functools.partial(_rwkv_kernel, tt=tt),
        grid=(batch, N_PAIRS, nt),
        in_specs=[blk(c_r), blk(c_r + N_PAIRS), blk(c_r + 2 * N_PAIRS),
                  pl.BlockSpec((tt, SMALL_W), lambda b, p, t: (b * nt + t, 6 * D_HEADS // SMALL_W)),
                  vec, vec, vec,
                  pl.BlockSpec((1, SMALL_W), lambda b, p, t: (0, 0)),
                  vec, vec, vec, vec, vec, vec, vec,
                  lora(PAIR), lora(PAIR), lora(GATE_LORA)],
        out_specs=pl.BlockSpec((1, tt, PAIR), lambda b, p, t: (b, t, p)),
        out_shape=jax.ShapeDtypeStruct((batch, seq, D_HEADS), _BF16),
        scratch_shapes=[pltpu.VMEM((8, PAIR), _F32), pltpu.VMEM((8, PAIR), _F32),
                        pltpu.VMEM((8, PAIR), _F32), pltpu.VMEM((8, SMALL_W), _F32),
                        pltpu.VMEM((PAIR, PAIR), _F32)],
        compiler_params=_cparams(("parallel", "parallel", "arbitrary"), 48),
        name="rwkv",
    )(pmat, pmat, pmat, pmat, prm["mu_r"], prm["mu_k"], prm["mu_v"], prm["mu_s"], prm["w0"],
      prm["a0"], prm["k_k"], prm["k_a"], prm["r_k"], prm["lnx_g"], prm["lnx_b"],
      prm["decay_b"], prm["a_b"], prm["g_b"])


def _outproj_kernel(fox_ref, rw_ref, x_ref, w1_ref, w2_ref, gt_ref, g2_ref, sh_ref, sc_ref,
                    x1_ref, h2_ref):
    y = (jnp.dot(fox_ref[...], w1_ref[...], preferred_element_type=_F32)
         + jnp.dot(rw_ref[...], w2_ref[...], preferred_element_type=_F32))
    x1 = x_ref[...] + gt_ref[0] * y
    x1_ref[...] = x1
    h = x1 * lax.rsqrt(jnp.mean(x1 * x1, axis=-1, keepdims=True) + NORM_EPS) * g2_ref[...]
    h2_ref[...] = (h * (1.0 + sc_ref[0]) + sh_ref[0]).astype(_BF16)


def _outproj(fox, rw, x2, w1, w2, gt, g2, sh, sc, seq):
    m, d = x2.shape
    tm = 256
    per_b = seq // tm
    rows = lambda w: pl.BlockSpec((tm, w), lambda i: (i, 0))
    full = lambda a: pl.BlockSpec(a.shape, lambda i: (0, 0))
    mod = pl.BlockSpec((1, 1, d), lambda i: (i // per_b, 0, 0))
    return pl.pallas_call(
        _outproj_kernel,
        grid=(m // tm,),
        in_specs=[rows(D_HEADS), rows(D_HEADS), rows(d), full(w1), full(w2), mod, full(g2), mod, mod],
        out_specs=[rows(d), rows(d)],
        out_shape=[jax.ShapeDtypeStruct((m, d), _F32), jax.ShapeDtypeStruct((m, d), _BF16)],
        compiler_params=_cparams(("parallel",), 48),
        name="outproj",
    )(fox, rw, x2, w1, w2, gt, g2, sh, sc)


def _mlp_kernel(h_ref, wu_ref, wd_ref, x1_ref, gt_ref, o_ref, acc_ref):
    f = pl.program_id(1)
    u = jnp.dot(h_ref[...], wu_ref[...], preferred_element_type=_F32)
    act = jnp.square(jnp.maximum(u, 0.0)).astype(_BF16)
    part = jnp.dot(act, wd_ref[...], preferred_element_type=_F32)

    @pl.when(f == 0)
    def _():
        acc_ref[...] = part

    @pl.when(f > 0)
    def _():
        acc_ref[...] += part

    @pl.when(f == pl.num_programs(1) - 1)
    def _():
        o_ref[...] = x1_ref[...] + gt_ref[0] * acc_ref[...]


def _mlp(h2, wu, wd, x1, gt, seq):
    m, d = x1.shape
    ff = wu.shape[1]
    tm, tf = 512, 512
    per_b = seq // tm
    return pl.pallas_call(
        _mlp_kernel,
        grid=(m // tm, ff // tf),
        in_specs=[pl.BlockSpec((tm, d), lambda i, f: (i, 0)),
                  pl.BlockSpec((d, tf), lambda i, f: (0, f)),
                  pl.BlockSpec((tf, d), lambda i, f: (f, 0)),
                  pl.BlockSpec((tm, d), lambda i, f: (i, 0)),
                  pl.BlockSpec((1, 1, d), lambda i, f: (i // per_b, 0, 0))],
        out_specs=pl.BlockSpec((tm, d), lambda i, f: (i, 0)),
        out_shape=jax.ShapeDtypeStruct((m, d), _F32),
        scratch_shapes=[pltpu.VMEM((tm, d), _F32)],
        compiler_params=_cparams(("parallel", "arbitrary"), 48),
        name="mlp",
    )(h2, wu, wd, x1, gt)


def _pad_rows(a, rows):
    return jnp.pad(a, ((0, rows - a.shape[0]), (0, 0)))


def _pad_lanes(a, width):
    return jnp.pad(a, ((0, 0), (0, width - a.shape[1])))


def _pack_w_in(w):
    d = w.shape[0]
    o_f = 3 * D_HEADS
    o_r = o_f + N_HEADS
    o_w = o_r + 3 * D_HEADS
    o_a = o_w + DECAY_LORA
    o_g = o_a + ICLR_LORA
    small = jnp.concatenate(
        [w[:, o_w:o_a], w[:, o_f:o_r], jnp.zeros((d, PAIR - DECAY_LORA - N_HEADS), w.dtype),
         w[:, o_a:o_g], jnp.zeros((d, PAIR - ICLR_LORA), w.dtype), w[:, o_g:]], axis=1)
    return jnp.concatenate([w[:, :o_f], w[:, o_r:o_w], small], axis=1).astype(_BF16)


def kernel(x, c, w_ada, b_ada, norm1_g, w_in, b_forget, q_norm_g, k_norm_g, fox_out_g, shift_mu, w0, decay_b, a0, a_b, g_b, k_k, k_a, r_k, lnx_g, lnx_b, w_out, norm2_g, w_mlp_up, w_mlp_down):
    batch, seq, d = x.shape
    depth = w_ada.shape[0]
    x2 = x.reshape(batch * seq, d)
    row = lambda a: a.reshape(1, -1)
    for l in range(depth):
        mod = _ada(c, w_ada[l], b_ada[l])
        sh1, sc1, gt1, sh2, sc2, gt2 = [m.reshape(batch, 1, d) for m in jnp.split(mod, 6, axis=-1)]

        pmat = _proj(x2, row(norm1_g[l]), sh1, sc1, _pack_w_in(w_in[l]), seq)

        bf_slot = jnp.pad(row(b_forget[l]), ((0, 0), (F_LANE0, PAIR - F_LANE0 - N_HEADS)))
        qa, ka, va = _fox_prep(pmat, jnp.tile(row(q_norm_g[l]), (1, 2)),
                               jnp.tile(row(k_norm_g[l]), (1, 2)), bf_slot, batch, seq)
        fox = _fox_attn(qa, ka, va, row(fox_out_g[l]), batch, seq)

        mu = row(shift_mu[l])
        o_w = 3 * D_HEADS
        o_a = o_w + DECAY_LORA
        o_g = o_a + ICLR_LORA
        prm = {
            "mu_r": mu[:, :D_HEADS], "mu_k": mu[:, D_HEADS:2 * D_HEADS], "mu_v": mu[:, 2 * D_HEADS:o_w],
            "mu_s": jnp.concatenate([_pad_lanes(mu[:, o_w:o_a], PAIR), _pad_lanes(mu[:, o_a:o_g], PAIR),
                                     mu[:, o_g:]], axis=1),
            "w0": row(w0[l]), "a0": row(a0[l]), "k_k": row(k_k[l]), "k_a": row(k_a[l]),
            "r_k": row(r_k[l]), "lnx_g": row(lnx_g[l]), "lnx_b": row(lnx_b[l]),
            "decay_b": _pad_rows(decay_b[l], PAIR).astype(_BF16),
            "a_b": _pad_rows(a_b[l], PAIR).astype(_BF16),
            "g_b": g_b[l].astype(_BF16),
        }
        rw = _rwkv(pmat, prm, batch, seq)

        w_o = w_out[l].astype(_BF16)
        x1, h2 = _outproj(fox.reshape(batch * seq, D_HEADS), rw.reshape(batch * seq, D_HEADS), x2,
                          w_o[:D_HEADS], w_o[D_HEADS:], gt1, row(norm2_g[l]), sh2, sc2, seq)
        x2 = _mlp(h2, w_mlp_up[l].astype(_BF16), w_mlp_down[l].astype(_BF16), x1, gt2, seq)
    return x2.reshape(batch, seq, d)
```

```python
import functools

import jax
import jax.numpy as jnp
from jax import lax
from jax.experimental import pallas as pl
from jax.experimental.pallas import tpu as pltpu

_F32 = jnp.float32
_BF16 = jnp.bfloat16

HEAD_DIM = 64
PAIR = 2 * HEAD_DIM
N_HEADS = 16
D_HEADS = N_HEADS * HEAD_DIM
N_PAIRS = N_HEADS // 2
DECAY_LORA = 96
ICLR_LORA = 96
GATE_LORA = 256
SMALL_W = 512
F_LANE0 = DECAY_LORA
NORM_EPS = 1e-6
LNX_EPS = 64e-5
NEG_INF = -1e30
LOG2E = 1.4426950408889634
CHUNK_LOG2 = 6
CHUNK = 1 << CHUNK_LOG2
MIB = 1024 * 1024


def _cparams(sem, vmem_mib):
    return pltpu.CompilerParams(dimension_semantics=sem, vmem_limit_bytes=vmem_mib * MIB)


def _dot(a, b):
    return jnp.dot(a.astype(_BF16), b.astype(_BF16), preferred_element_type=_F32)


def _dot_nt(a, b):
    return lax.dot_general(a.astype(_BF16), b.astype(_BF16), (((1,), (1,)), ((), ())),
                           preferred_element_type=_F32)


def _dot_tn(a, b):
    return lax.dot_general(a.astype(_BF16), b.astype(_BF16), (((0,), (0,)), ((), ())),
                           preferred_element_type=_F32)


def _split3(x):
    h = x.astype(_BF16).astype(_F32)
    r = x - h
    m = r.astype(_BF16).astype(_F32)
    l = (r - m).astype(_BF16).astype(_F32)
    return h, m, l


def _dot_exact_rhs(a_bf16, x):
    return sum(jnp.dot(a_bf16, p.astype(_BF16), preferred_element_type=_F32) for p in _split3(x))


def _sigmoid(x):
    return 1.0 / (1.0 + jnp.exp(-x))


def _softplus(x):
    return jnp.maximum(x, 0.0) + jnp.log(1.0 + jnp.exp(-jnp.abs(x)))


def _half_sum(x, lo_half):
    s_lo = jnp.sum(jnp.where(lo_half, x, 0.0), axis=-1, keepdims=True)
    s_hi = jnp.sum(jnp.where(lo_half, 0.0, x), axis=-1, keepdims=True)
    return jnp.where(lo_half, s_lo, s_hi)


def _lo_half_mask():
    return lax.broadcasted_iota(jnp.int32, (1, PAIR), 1) < HEAD_DIM


def _ada_kernel(c_ref, w_ref, b_ref, o_ref):
    c = c_ref[...]
    cond = c * _sigmoid(c)
    o_ref[...] = _dot(cond, w_ref[...]) + b_ref[...]


def _ada(c, w_ada, b_ada):
    b, d = c.shape
    n = w_ada.shape[1]
    rows = 8 * pl.cdiv(b, 8)
    tn = 1024
    cp = jnp.pad(c, ((0, rows - b), (0, 0)))
    out = pl.pallas_call(
        _ada_kernel,
        grid=(n // tn,),
        in_specs=[pl.BlockSpec((rows, d), lambda j: (0, 0)),
                  pl.BlockSpec((d, tn), lambda j: (0, j)),
                  pl.BlockSpec((1, tn), lambda j: (0, j))],
        out_specs=pl.BlockSpec((rows, tn), lambda j: (0, j)),
        out_shape=jax.ShapeDtypeStruct((rows, n), _F32),
        compiler_params=_cparams(("parallel",), 40),
        name="ada",
    )(cp, w_ada, b_ada.reshape(1, n))
    return out[:b]


def _proj_kernel(x_ref, g_ref, sh_ref, sc_ref, w_ref, mu_ref, o_ref, h_ref, carry_ref, *, tm, per_b):
    i = pl.program_id(0)
    j = pl.program_id(1)

    @pl.when(j == 0)
    def _():
        x = x_ref[...]
        y = x * lax.rsqrt(jnp.mean(x * x, axis=-1, keepdims=True) + NORM_EPS) * g_ref[...]
        h_ref[...] = (y * (1.0 + sc_ref[0]) + sh_ref[0]).astype(_BF16)

    y = jnp.dot(h_ref[...], w_ref[...], preferred_element_type=_F32)
    first_row = lax.broadcasted_iota(jnp.int32, (tm, 1), 0) == 0
    above = jnp.where(lax.rem(i, per_b) == 0, 0.0, carry_ref[j, 0:1, :])
    prev = jnp.where(first_row, above, pltpu.roll(y, 1, axis=0))
    carry_ref[j, 0:1, :] = y[tm - 1:tm, :]
    o_ref[...] = y + (prev - y) * mu_ref[...]


def _proj(x2, g, sh, sc, w, mu, seq):
    m, d = x2.shape
    n = w.shape[1]
    tm, tn = 1024, 512
    per_b = seq // tm
    return pl.pallas_call(
        functools.partial(_proj_kernel, tm=tm, per_b=per_b),
        grid=(m // tm, n // tn),
        in_specs=[pl.BlockSpec((tm, d), lambda i, j: (i, 0)),
                  pl.BlockSpec((1, d), lambda i, j: (0, 0)),
                  pl.BlockSpec((1, 1, d), lambda i, j: (i // per_b, 0, 0)),
                  pl.BlockSpec((1, 1, d), lambda i, j: (i // per_b, 0, 0)),
                  pl.BlockSpec((d, tn), lambda i, j: (0, j)),
                  pl.BlockSpec((1, tn), lambda i, j: (0, j))],
        out_specs=pl.BlockSpec((tm, tn), lambda i, j: (i, j)),
        out_shape=jax.ShapeDtypeStruct((m, n), _F32),
        scratch_shapes=[pltpu.VMEM((tm, d), _BF16), pltpu.VMEM((n // tn, 8, tn), _F32)],
        compiler_params=_cparams(("arbitrary", "arbitrary"), 48),
        name="proj",
    )(x2, g, sh, sc, w, mu)


def _fox_prep_kernel(q_ref, k_ref, v_ref, s_ref, qg_ref, kg_ref, bf_ref, qt_ref, ka_ref, vt_ref,
                     lo_ref, carry_ref, *, tt):
    @pl.when(pl.program_id(1) == 0)
    def _():
        carry_ref[...] = jnp.zeros_like(carry_ref)

    lane = lax.broadcasted_iota(jnp.int32, (1, PAIR), 1)
    lo_half = lane < HEAD_DIM

    lo_ref[:, 0:PAIR] = jnp.tanh(s_ref[:, 0:PAIR]).astype(_BF16)
    lo_ref[:, PAIR:2 * PAIR] = s_ref[:, PAIR:2 * PAIR].astype(_BF16)
    lo_ref[:, 2 * PAIR:] = _sigmoid(s_ref[:, 2 * PAIR:]).astype(_BF16)

    z = s_ref[:, 0:PAIR] + bf_ref[...]
    logf = -_softplus(-z)
    row = lax.broadcasted_iota(jnp.int32, (tt, tt), 0)
    col = lax.broadcasted_iota(jnp.int32, (tt, tt), 1)
    tri = jnp.where(row >= col, 1.0, 0.0).astype(_BF16)
    dcum = carry_ref[0:1, :] + _dot_exact_rhs(tri, logf)
    carry_ref[0:1, :] = dcum[tt - 1:tt, :]
    dcum2 = dcum * LOG2E

    for p in range(N_PAIRS):
        cols = slice(p * PAIR, (p + 1) * PAIR)
        q2 = q_ref[:, cols]
        k2 = k_ref[:, cols]
        v2 = v_ref[:, cols]
        qn = q2 * lax.rsqrt(_half_sum(q2 * q2, lo_half) * (1.0 / HEAD_DIM) + NORM_EPS) * qg_ref[...]
        qn = qn * (HEAD_DIM ** -0.5 * LOG2E)
        kn = k2 * lax.rsqrt(_half_sum(k2 * k2, lo_half) * (1.0 / HEAD_DIM) + NORM_EPS) * kg_ref[...]
        for hh in range(2):
            h = 2 * p + hh
            d1, d2, d3 = _split3(dcum2[:, F_LANE0 + h:F_LANE0 + h + 1])
            base = HEAD_DIM if hh == 0 else 0
            keep = lo_half if hh == 0 else jnp.logical_not(lo_half)
            aq = jnp.where(lane < base + 3, 1.0,
                           jnp.where(lane == base + 3, d1,
                                     jnp.where(lane == base + 4, d2,
                                               jnp.where(lane == base + 5, d3, 0.0))))
            ak = jnp.where(lane == base, -d1,
                           jnp.where(lane == base + 1, -d2,
                                     jnp.where(lane == base + 2, -d3,
                                               jnp.where(lane < base + 6, 1.0, 0.0))))
            av = jnp.where(lane == base, 1.0, 0.0)
            qt_ref[0, h] = jnp.where(keep, qn, aq).T.astype(_BF16)
            ka_ref[0, h] = jnp.where(keep, kn, ak).astype(_BF16)
            vt_ref[0, h] = jnp.where(keep, v2, av).T.astype(_BF16)


def _fox_prep(pmat, qg2, kg2, bf_slot, batch, seq):
    tt = 256
    nt = seq // tt
    row_map = lambda c: (lambda b, t: (b * nt + t, c))
    row_sd = jax.ShapeDtypeStruct((batch, N_HEADS, seq, PAIR), _BF16)
    col_sd = jax.ShapeDtypeStruct((batch, N_HEADS, PAIR, seq), _BF16)
    row_spec = pl.BlockSpec((1, N_HEADS, tt, PAIR), lambda b, t: (b, 0, t, 0))
    col_spec = pl.BlockSpec((1, N_HEADS, PAIR, tt), lambda b, t: (b, 0, 0, t))
    vec = pl.BlockSpec((1, PAIR), lambda b, t: (0, 0))
    return pl.pallas_call(
        functools.partial(_fox_prep_kernel, tt=tt),
        grid=(batch, nt),
        in_specs=[pl.BlockSpec((tt, D_HEADS), row_map(0)),
                  pl.BlockSpec((tt, D_HEADS), row_map(1)),
                  pl.BlockSpec((tt, D_HEADS), row_map(2)),
                  pl.BlockSpec((tt, SMALL_W), row_map(6 * D_HEADS // SMALL_W)),
                  vec, vec, vec],
        out_specs=[col_spec, row_spec, col_spec, pl.BlockSpec((tt, SMALL_W), row_map(0))],
        out_shape=[col_sd, row_sd, col_sd, jax.ShapeDtypeStruct((batch * seq, SMALL_W), _BF16)],
        scratch_shapes=[pltpu.VMEM((8, PAIR), _F32)],
        compiler_params=_cparams(("parallel", "arbitrary"), 40),
        name="fox_prep",
    )(pmat, pmat, pmat, pmat, qg2, kg2, bf_slot)


def _fox_attn_kernel(qt_ref, k_ref, vt_ref, g_ref, o_ref, m_ref, acc_ref, *, tq):
    qi = pl.program_id(2)
    key = lax.broadcasted_iota(jnp.int32, (tq, tq), 0)
    qry = lax.broadcasted_iota(jnp.int32, (tq, tq), 1)
    m_ref[...] = jnp.full(m_ref.shape, NEG_INF, _F32)
    acc_ref[...] = jnp.zeros(acc_ref.shape, _F32)

    def step(off, causal):
        for hh in range(2):
            k = k_ref[0, hh, pl.ds(off, tq), :]
            vt = vt_ref[0, hh, :, pl.ds(off, tq)]
            st = jnp.dot(k, qt_ref[0, hh], preferred_element_type=_F32)
            if causal:
                st = jnp.where(key <= qry, st, NEG_INF)
            m_old = m_ref[hh]
            m_new = jnp.maximum(m_old, jnp.max(st, axis=0, keepdims=True))
            p = jnp.exp2(st - m_new)
            acc_ref[hh] = jnp.exp2(m_old - m_new) * acc_ref[hh] + jnp.dot(
                vt, p.astype(_BF16), preferred_element_type=_F32)
            m_ref[hh] = m_new

    def body(ki, carry):
        step(pl.multiple_of(ki * tq, tq), False)
        return carry

    lax.fori_loop(0, qi, body, 0)
    step(pl.multiple_of(qi * tq, tq), True)

    acc0, acc1 = acc_ref[0], acc_ref[1]
    o0 = acc0[:HEAD_DIM] / acc0[HEAD_DIM:HEAD_DIM + 1]
    o1 = acc1[HEAD_DIM:] / acc1[0:1]
    n0 = o0 * lax.rsqrt(jnp.mean(o0 * o0, axis=0, keepdims=True) + NORM_EPS)
    n1 = o1 * lax.rsqrt(jnp.mean(o1 * o1, axis=0, keepdims=True) + NORM_EPS)
    o_ref[0] = (jnp.concatenate([n0, n1], axis=0).T * g_ref[...]).astype(_BF16)


def _fox_attn(qt, ka, vt, og, batch, seq):
    tq = 512
    nq = seq // tq
    return pl.pallas_call(
        functools.partial(_fox_attn_kernel, tq=tq),
        grid=(batch, N_PAIRS, nq),
        in_specs=[pl.BlockSpec((1, 2, PAIR, tq), lambda b, p, i: (b, p, 0, i)),
                  pl.BlockSpec((1, 2, seq, PAIR), lambda b, p, i: (b, p, 0, 0)),
                  pl.BlockSpec((1, 2, PAIR, seq), lambda b, p, i: (b, p, 0, 0)),
                  pl.BlockSpec((1, PAIR), lambda b, p, i: (0, p))],
        out_specs=pl.BlockSpec((1, tq, PAIR), lambda b, p, i: (b, i, p)),
        out_shape=jax.ShapeDtypeStruct((batch, seq, D_HEADS), _BF16),
        scratch_shapes=[pltpu.VMEM((2, 1, tq), _F32), pltpu.VMEM((2, PAIR, tq), _F32)],
        compiler_params=_cparams(("parallel", "parallel", "arbitrary"), 40),
        name="fox_attn",
    )(qt, ka, vt, og)


def _stack_heads(x, lo_half):
    return jnp.concatenate([jnp.where(lo_half, x, 0.0), jnp.where(lo_half, 0.0, x)], axis=0)


def _interleave(*gens):
    live = list(gens)
    while live:
        for gen in list(live):
            try:
                next(gen)
            except StopIteration:
                live.remove(gen)


def _rwkv_kernel(r_ref, k_ref, v_ref, lo_ref, w0_ref, a0_ref, kk_ref, ka_ref, rk_ref, lg_ref, lb_ref,
                 db_ref, ab_ref, gb_ref, o_ref, z_ref, lhs_ref, add_ref, dec_ref, bv_ref, gate_ref,
                 *, tt, nt):
    nc = tt // CHUNK
    c2 = 2 * CHUNK
    t = pl.program_id(2)
    wr = lax.rem(t, 2)
    rd = 1 - wr
    lo_half = _lo_half_mask()

    @pl.when(t == 0)
    def _():
        z_ref[...] = jnp.zeros_like(z_ref)
        lhs_ref[1] = jnp.zeros(lhs_ref.shape[1:], lhs_ref.dtype)
        add_ref[1] = jnp.zeros(add_ref.shape[1:], add_ref.dtype)
        dec_ref[1] = jnp.ones(dec_ref.shape[1:], dec_ref.dtype)
        bv_ref[1] = jnp.zeros(bv_ref.shape[1:], bv_ref.dtype)
        gate_ref[1] = jnp.zeros(gate_ref.shape[1:], gate_ref.dtype)

    def advance():
        z = z_ref[...]
        dec = dec_ref[rd]
        ys = []
        for c in range(nc):
            res = jnp.dot(lhs_ref[rd, c], z.astype(_BF16), preferred_element_type=_F32)
            add = add_ref[rd, c]
            ysh = res[:c2] + add[:c2]
            ys.append(ysh[:CHUNK] + ysh[CHUNK:])
            z = dec[:, c:c + 1] * z + res[c2:] + add[c2:]
            yield
        z_ref[...] = z
        y = jnp.concatenate(ys, axis=0)
        mu = _half_sum(y, lo_half) * (1.0 / HEAD_DIM)
        dlt = y - mu
        var = _half_sum(dlt * dlt, lo_half) * (1.0 / HEAD_DIM)
        yn = dlt * lax.rsqrt(var + LNX_EPS) * lg_ref[...] + lb_ref[...]
        o_ref[0] = ((yn + bv_ref[rd]) * gate_ref[rd]).astype(_BF16)
        yield

    def prepare():
        r = r_ref[...]
        k = k_ref[...]
        v = v_ref[...]
        w = -_softplus(-(w0_ref[...] + jnp.dot(lo_ref[:, 0:PAIR], db_ref[...],
                                                 preferred_element_type=_F32))) - 0.5
        logd = -jnp.exp(w)
        a = _sigmoid(a0_ref[...] + jnp.dot(lo_ref[:, PAIR:2 * PAIR], ab_ref[...],
                                           preferred_element_type=_F32))
        gate_ref[wr] = jnp.dot(lo_ref[:, 2 * PAIR:], gb_ref[...], preferred_element_type=_F32)
        kk = k * kk_ref[...]
        kk = kk * lax.rsqrt(jnp.maximum(_half_sum(kk * kk, lo_half), 1e-24))
        k = k * (1.0 + (a - 1.0) * ka_ref[...])
        bb = kk * a
        bv_ref[wr] = _half_sum(r * k * rk_ref[...], lo_half) * v
        yield

        pos = lax.broadcasted_iota(jnp.int32, (tt, 1), 0) & (CHUNK - 1)
        lcum = logd
        for j in range(CHUNK_LOG2):
            lcum = lcum + jnp.where(pos >= (1 << j), pltpu.roll(lcum, 1 << j, axis=0), 0.0)
        yield

        r2 = lax.broadcasted_iota(jnp.int32, (c2, c2), 0)
        q2 = lax.broadcasted_iota(jnp.int32, (c2, c2), 1)
        same = (r2 >> CHUNK_LOG2) == (q2 >> CHUNK_LOG2)
        strict = same & (r2 > q2)
        incl = same & (r2 >= q2)
        eye = jnp.where(r2 == q2, 1.0, 0.0)
        zeros_cc = jnp.zeros((c2, PAIR), _F32)

        rng = range(nc)
        sl = [slice(c * CHUNK, (c + 1) * CHUNK) for c in rng]
        lc = [lcum[sl[c]] for c in rng]
        lc_rows = [lc[c][CHUNK - 1:CHUNK, :] for c in rng]
        dec_ref[wr] = jnp.exp(jnp.concatenate(lc_rows + [jnp.zeros((PAIR - nc, PAIR), _F32)], axis=0).T)
        e_pos = [jnp.exp(lc[c]) for c in rng]
        e_neg = [jnp.exp(-lc[c]) for c in rng]
        e_end = [jnp.exp(lc_rows[c] - lc[c]) for c in rng]
        rp = [_stack_heads(r[sl[c]] * e_pos[c], lo_half) for c in rng]
        ap = [_stack_heads(-kk[sl[c]] * jnp.exp(lc[c] - logd[sl[c]]), lo_half) for c in rng]
        kq = [_stack_heads(k[sl[c]] * e_neg[c], lo_half) for c in rng]
        bq = [_stack_heads(bb[sl[c]] * e_neg[c], lo_half) for c in rng]
        kc = [_stack_heads(k[sl[c]] * e_end[c], lo_half) for c in rng]
        bc = [_stack_heads(bb[sl[c]] * e_end[c], lo_half) for c in rng]
        vs = [_stack_heads(v[sl[c]], lo_half) for c in rng]
        yield

        s = [_dot_nt(jnp.concatenate([ap[c], rp[c]], axis=0), jnp.concatenate([bq[c], kq[c]], axis=0))
             for c in rng]
        a_ab = [jnp.where(strict, s[c][:c2, :c2], 0.0) for c in rng]
        a_ak = [jnp.where(strict, s[c][:c2, c2:], 0.0) for c in rng]
        a_rb = [jnp.where(incl, s[c][c2:, :c2], 0.0) for c in rng]
        a_rk = [jnp.where(incl, s[c][c2:, c2:], 0.0) for c in rng]
        yield
        akv = [_dot(a_ak[c], vs[c]) for c in rng]

        pw = [_dot(a_ab[c], a_ab[c]) for c in rng]
        tinv = [eye + a_ab[c] for c in rng]
        yield
        for _ in range(CHUNK_LOG2 - 2):
            pt = [_dot(pw[c], jnp.concatenate([pw[c], tinv[c]], axis=1)) for c in rng]
            pw = [pt[c][:, :c2] for c in rng]
            tinv = [tinv[c] + pt[c][:, c2:] for c in rng]
            yield
        q5 = [_dot(tinv[c], jnp.concatenate([ap[c], akv[c]], axis=1)) for c in rng]
        yield
        wu = [q5[c] + _dot(pw[c], q5[c]) for c in rng]
        yield
        x = [jnp.concatenate([wu[c], jnp.concatenate([zeros_cc, vs[c]], axis=1)], axis=0) for c in rng]
        ry = [_dot(jnp.concatenate([a_rb[c], a_rk[c]], axis=1), x[c]) for c in rng]
        mz = [_dot_tn(jnp.concatenate([bc[c], kc[c]], axis=0), x[c]) for c in rng]
        yield
        for c in rng:
            lhs_ref[wr, c] = jnp.concatenate([rp[c] + ry[c][:, :PAIR], mz[c][:, :PAIR]], axis=0).astype(_BF16)
            add_ref[wr, c] = jnp.concatenate([ry[c][:, PAIR:], mz[c][:, PAIR:]], axis=0)
        yield

    @pl.when(t < nt)
    def _():
        _interleave(prepare(), advance())

    @pl.when(t == nt)
    def _():
        _interleave(advance())


def _rwkv(pmat, lora_in, prm, batch, seq):
    tt = 512
    nt = seq // tt
    nc = tt // CHUNK
    cur = lambda b, t: b * nt + jnp.minimum(t, nt - 1)
    blk = lambda c0: pl.BlockSpec((tt, PAIR), lambda b, p, t, c0=c0: (cur(b, t), c0 + p))
    vec = pl.BlockSpec((1, PAIR), lambda b, p, t: (0, p))
    lora = lambda rows: pl.BlockSpec((rows, PAIR), lambda b, p, t: (0, p))
    c_r = 3 * D_HEADS // PAIR
    return pl.pallas_call(
        functools.partial(_rwkv_kernel, tt=tt, nt=nt),
        grid=(batch, N_PAIRS, nt + 1),
        in_specs=[blk(c_r), blk(c_r + N_PAIRS), blk(c_r + 2 * N_PAIRS),
                  pl.BlockSpec((tt, SMALL_W), lambda b, p, t: (cur(b, t), 0)),
                  vec, vec, vec, vec, vec, vec, vec,
                  lora(PAIR), lora(PAIR), lora(GATE_LORA)],
        out_specs=pl.BlockSpec((1, tt, PAIR), lambda b, p, t: (b, jnp.maximum(t - 1, 0), p)),
        out_shape=jax.ShapeDtypeStruct((batch, seq, D_HEADS), _BF16),
        scratch_shapes=[pltpu.VMEM((PAIR, PAIR), _F32),
                        pltpu.VMEM((2, nc, 2 * PAIR, PAIR), _BF16),
                        pltpu.VMEM((2, nc, 2 * PAIR, PAIR), _F32),
                        pltpu.VMEM((2, PAIR, PAIR), _F32),
                        pltpu.VMEM((2, tt, PAIR), _F32),
                        pltpu.VMEM((2, tt, PAIR), _F32)],
        compiler_params=_cparams(("parallel", "parallel", "arbitrary"), 48),
        name="rwkv",
    )(pmat, pmat, pmat, lora_in, prm["w0"], prm["a0"], prm["k_k"], prm["k_a"], prm["r_k"],
      prm["lnx_g"], prm["lnx_b"], prm["decay_b"], prm["a_b"], prm["g_b"])


def _outproj_kernel(fox_ref, rw_ref, x_ref, w1_ref, w2_ref, gt_ref, g2_ref, sh_ref, sc_ref,
                    x1_ref, h2_ref):
    y = (jnp.dot(fox_ref[...], w1_ref[...], preferred_element_type=_F32)
         + jnp.dot(rw_ref[...], w2_ref[...], preferred_element_type=_F32))
    x1 = x_ref[...] + gt_ref[0] * y
    x1_ref[...] = x1
    h = x1 * lax.rsqrt(jnp.mean(x1 * x1, axis=-1, keepdims=True) + NORM_EPS) * g2_ref[...]
    h2_ref[...] = (h * (1.0 + sc_ref[0]) + sh_ref[0]).astype(_BF16)


def _outproj(fox, rw, x2, w1, w2, gt, g2, sh, sc, seq):
    m, d = x2.shape
    tm = 256
    per_b = seq // tm
    rows = lambda w: pl.BlockSpec((tm, w), lambda i: (i, 0))
    full = lambda a: pl.BlockSpec(a.shape, lambda i: (0, 0))
    mod = pl.BlockSpec((1, 1, d), lambda i: (i // per_b, 0, 0))
    return pl.pallas_call(
        _outproj_kernel,
        grid=(m // tm,),
        in_specs=[rows(D_HEADS), rows(D_HEADS), rows(d), full(w1), full(w2), mod, full(g2), mod, mod],
        out_specs=[rows(d), rows(d)],
        out_shape=[jax.ShapeDtypeStruct((m, d), _F32), jax.ShapeDtypeStruct((m, d), _BF16)],
        compiler_params=_cparams(("parallel",), 48),
        name="outproj",
    )(fox, rw, x2, w1, w2, gt, g2, sh, sc)


def _mlp_kernel(h_ref, wu_ref, wd_ref, x1_ref, gt_ref, o_ref, acc_ref):
    f = pl.program_id(1)

    @pl.when(f == 0)
    def _():
        acc_ref[...] = jnp.zeros_like(acc_ref)

    u = jnp.dot(h_ref[...], wu_ref[...], preferred_element_type=_F32)
    act = jnp.square(jnp.maximum(u, 0.0)).astype(_BF16)
    acc_ref[...] += jnp.dot(act, wd_ref[...], preferred_element_type=_F32)

    @pl.when(f == pl.num_programs(1) - 1)
    def _():
        o_ref[...] = x1_ref[...] + gt_ref[0] * acc_ref[...]


def _mlp(h2, wu, wd, x1, gt, seq):
    m, d = x1.shape
    ff = wu.shape[1]
    tm, tf = 1024, 512
    per_b = seq // tm
    rows = lambda: pl.BlockSpec((tm, d), lambda i, f: (i, 0), pipeline_mode=pl.Buffered(1))
    return pl.pallas_call(
        _mlp_kernel,
        grid=(m // tm, ff // tf),
        in_specs=[rows(),
                  pl.BlockSpec((d, tf), lambda i, f: (0, f)),
                  pl.BlockSpec((tf, d), lambda i, f: (f, 0)),
                  rows(),
                  pl.BlockSpec((1, 1, d), lambda i, f: (i // per_b, 0, 0))],
        out_specs=rows(),
        out_shape=jax.ShapeDtypeStruct((m, d), _F32),
        scratch_shapes=[pltpu.VMEM((tm, d), _F32)],
        compiler_params=_cparams(("parallel", "arbitrary"), 56),
        name="mlp",
    )(h2, wu, wd, x1, gt)


def _pad_rows(a, rows):
    return jnp.pad(a, ((0, rows - a.shape[0]), (0, 0)))


def _pad_lanes(a, width):
    return jnp.pad(a, ((0, 0), (0, width - a.shape[1])))


def _pack_w_in(w):
    d = w.shape[0]
    o_f = 3 * D_HEADS
    o_r = o_f + N_HEADS
    o_w = o_r + 3 * D_HEADS
    o_a = o_w + DECAY_LORA
    o_g = o_a + ICLR_LORA
    small = jnp.concatenate(
        [w[:, o_w:o_a], w[:, o_f:o_r], jnp.zeros((d, PAIR - DECAY_LORA - N_HEADS), w.dtype),
         w[:, o_a:o_g], jnp.zeros((d, PAIR - ICLR_LORA), w.dtype), w[:, o_g:]], axis=1)
    return jnp.concatenate([w[:, :o_f], w[:, o_r:o_w], small], axis=1).astype(_BF16)


def _pack_shift_mu(mu):
    o_w = 3 * D_HEADS
    o_a = o_w + DECAY_LORA
    o_g = o_a + ICLR_LORA
    return jnp.concatenate([jnp.zeros((1, 3 * D_HEADS), _F32), mu[:, :o_w], _pad_lanes(mu[:, o_w:o_a], PAIR),
                            _pad_lanes(mu[:, o_a:o_g], PAIR), mu[:, o_g:]], axis=1)


def kernel(x, c, w_ada, b_ada, norm1_g, w_in, b_forget, q_norm_g, k_norm_g, fox_out_g, shift_mu, w0, decay_b, a0, a_b, g_b, k_k, k_a, r_k, lnx_g, lnx_b, w_out, norm2_g, w_mlp_up, w_mlp_down):
    batch, seq, d = x.shape
    depth = w_ada.shape[0]
    x2 = x.reshape(batch * seq, d)
    row = lambda a: a.reshape(1, -1)
    for l in range(depth):
        mod = _ada(c, w_ada[l], b_ada[l])
        sh1, sc1, gt1, sh2, sc2, gt2 = [m.reshape(batch, 1, d) for m in jnp.split(mod, 6, axis=-1)]

        pmat = _proj(x2, row(norm1_g[l]), sh1, sc1, _pack_w_in(w_in[l]), _pack_shift_mu(row(shift_mu[l])), seq)

        bf_slot = jnp.pad(row(b_forget[l]), ((0, 0), (F_LANE0, PAIR - F_LANE0 - N_HEADS)))
        qt, ka, vt, lora_in = _fox_prep(pmat, jnp.tile(row(q_norm_g[l]), (1, 2)),
                                        jnp.tile(row(k_norm_g[l]), (1, 2)), bf_slot, batch, seq)
        fox = _fox_attn(qt, ka, vt, row(fox_out_g[l]), batch, seq)

        prm = {
            "w0": row(w0[l]), "a0": row(a0[l]), "k_k": row(k_k[l]), "k_a": row(k_a[l]),
            "r_k": row(r_k[l]), "lnx_g": row(lnx_g[l]), "lnx_b": row(lnx_b[l]),
            "decay_b": _pad_rows(decay_b[l], PAIR).astype(_BF16),
            "a_b": _pad_rows(a_b[l], PAIR).astype(_BF16),
            "g_b": g_b[l].astype(_BF16),
        }
        rw = _rwkv(pmat, lora_in, prm, batch, seq)

        w_o = w_out[l].astype(_BF16)
        x1, h2 = _outproj(fox.reshape(batch * seq, D_HEADS), rw.reshape(batch * seq, D_HEADS), x2,
                          w_o[:D_HEADS], w_o[D_HEADS:], gt1, row(norm2_g[l]), sh2, sc2, seq)
        x2 = _mlp(h2, w_mlp_up[l].astype(_BF16), w_mlp_down[l].astype(_BF16), x1, gt2, seq)
    return x2.reshape(batch, seq, d)
```

```python
import functools

import jax
import jax.numpy as jnp
from jax import lax
from jax.experimental import pallas as pl
from jax.experimental.pallas import tpu as pltpu

_F32 = jnp.float32
_BF16 = jnp.bfloat16

HEAD_DIM = 64
PAIR = 2 * HEAD_DIM
N_HEADS = 16
D_HEADS = N_HEADS * HEAD_DIM
N_PAIRS = N_HEADS // 2
DECAY_LORA = 96
ICLR_LORA = 96
GATE_LORA = 256
SMALL_W = 512
F_LANE0 = DECAY_LORA
NORM_EPS = 1e-6
LNX_EPS = 64e-5
NEG_INF = -1e30
LOG2E = 1.4426950408889634
CHUNK_LOG2 = 6
CHUNK = 1 << CHUNK_LOG2
MIB = 1024 * 1024


def _cparams(sem, vmem_mib):
    return pltpu.CompilerParams(dimension_semantics=sem, vmem_limit_bytes=vmem_mib * MIB)


def _dot(a, b):
    return jnp.dot(a.astype(_BF16), b.astype(_BF16), preferred_element_type=_F32)


def _dot_nt(a, b):
    return lax.dot_general(a.astype(_BF16), b.astype(_BF16), (((1,), (1,)), ((), ())),
                           preferred_element_type=_F32)


def _dot_tn(a, b):
    return lax.dot_general(a.astype(_BF16), b.astype(_BF16), (((0,), (0,)), ((), ())),
                           preferred_element_type=_F32)


def _split3(x):
    h = x.astype(_BF16).astype(_F32)
    r = x - h
    m = r.astype(_BF16).astype(_F32)
    l = (r - m).astype(_BF16).astype(_F32)
    return h, m, l


def _dot_exact_rhs(a_bf16, x):
    return sum(jnp.dot(a_bf16, p.astype(_BF16), preferred_element_type=_F32) for p in _split3(x))


def _sigmoid(x):
    return 1.0 / (1.0 + jnp.exp(-x))


def _softplus(x):
    return jnp.maximum(x, 0.0) + jnp.log(1.0 + jnp.exp(-jnp.abs(x)))


def _half_sum(x, lo_half):
    s_lo = jnp.sum(jnp.where(lo_half, x, 0.0), axis=-1, keepdims=True)
    s_hi = jnp.sum(jnp.where(lo_half, 0.0, x), axis=-1, keepdims=True)
    return jnp.where(lo_half, s_lo, s_hi)


def _lo_half_mask():
    return lax.broadcasted_iota(jnp.int32, (1, PAIR), 1) < HEAD_DIM


def _ada_kernel(c_ref, w_ref, b_ref, o_ref):
    c = c_ref[...]
    cond = c * _sigmoid(c)
    o_ref[...] = _dot(cond, w_ref[...]) + b_ref[...]


def _ada(c, w_ada, b_ada):
    b, d = c.shape
    n = w_ada.shape[1]
    rows = 8 * pl.cdiv(b, 8)
    tn = 1024
    cp = jnp.pad(c, ((0, rows - b), (0, 0)))
    out = pl.pallas_call(
        _ada_kernel,
        grid=(n // tn,),
        in_specs=[pl.BlockSpec((rows, d), lambda j: (0, 0)),
                  pl.BlockSpec((d, tn), lambda j: (0, j)),
                  pl.BlockSpec((1, tn), lambda j: (0, j))],
        out_specs=pl.BlockSpec((rows, tn), lambda j: (0, j)),
        out_shape=jax.ShapeDtypeStruct((rows, n), _F32),
        compiler_params=_cparams(("parallel",), 40),
        name="ada",
    )(cp, w_ada, b_ada.reshape(1, n))
    return out[:b]


def _proj_kernel(x_ref, g_ref, sh_ref, sc_ref, w_ref, mu_ref, o_ref, h_ref, carry_ref, *, tm, per_b):
    i = pl.program_id(0)
    j = pl.program_id(1)

    @pl.when(j == 0)
    def _():
        x = x_ref[...]
        y = x * lax.rsqrt(jnp.mean(x * x, axis=-1, keepdims=True) + NORM_EPS) * g_ref[...]
        h_ref[...] = (y * (1.0 + sc_ref[0]) + sh_ref[0]).astype(_BF16)

    y = jnp.dot(h_ref[...], w_ref[...], preferred_element_type=_F32)
    first_row = lax.broadcasted_iota(jnp.int32, (tm, 1), 0) == 0
    above = jnp.where(lax.rem(i, per_b) == 0, 0.0, carry_ref[j, 0:1, :])
    prev = jnp.where(first_row, above, pltpu.roll(y, 1, axis=0))
    carry_ref[j, 0:1, :] = y[tm - 1:tm, :]
    o_ref[...] = y + (prev - y) * mu_ref[...]


def _proj(x2, g, sh, sc, w, mu, seq):
    m, d = x2.shape
    n = w.shape[1]
    tm, tn = 1024, 512
    per_b = seq // tm
    return pl.pallas_call(
        functools.partial(_proj_kernel, tm=tm, per_b=per_b),
        grid=(m // tm, n // tn),
        in_specs=[pl.BlockSpec((tm, d), lambda i, j: (i, 0)),
                  pl.BlockSpec((1, d), lambda i, j: (0, 0)),
                  pl.BlockSpec((1, 1, d), lambda i, j: (i // per_b, 0, 0)),
                  pl.BlockSpec((1, 1, d), lambda i, j: (i // per_b, 0, 0)),
                  pl.BlockSpec((d, tn), lambda i, j: (0, j)),
                  pl.BlockSpec((1, tn), lambda i, j: (0, j))],
        out_specs=pl.BlockSpec((tm, tn), lambda i, j: (i, j)),
        out_shape=jax.ShapeDtypeStruct((m, n), _F32),
        scratch_shapes=[pltpu.VMEM((tm, d), _BF16), pltpu.VMEM((n // tn, 8, tn), _F32)],
        compiler_params=_cparams(("arbitrary", "arbitrary"), 48),
        name="proj",
    )(x2, g, sh, sc, w, mu)


def _fox_prep_kernel(q_ref, k_ref, v_ref, s_ref, qg_ref, kg_ref, bf_ref, qa_ref, ka_ref, vt_ref,
                     lo_ref, carry_ref, *, tt):
    @pl.when(pl.program_id(1) == 0)
    def _():
        carry_ref[...] = jnp.zeros_like(carry_ref)

    lane = lax.broadcasted_iota(jnp.int32, (1, PAIR), 1)
    lo_half = lane < HEAD_DIM

    lo_ref[:, 0:PAIR] = jnp.tanh(s_ref[:, 0:PAIR]).astype(_BF16)
    lo_ref[:, PAIR:2 * PAIR] = s_ref[:, PAIR:2 * PAIR].astype(_BF16)
    lo_ref[:, 2 * PAIR:] = _sigmoid(s_ref[:, 2 * PAIR:]).astype(_BF16)

    z = s_ref[:, 0:PAIR] + bf_ref[...]
    logf = -_softplus(-z)
    row = lax.broadcasted_iota(jnp.int32, (tt, tt), 0)
    col = lax.broadcasted_iota(jnp.int32, (tt, tt), 1)
    tri = jnp.where(row >= col, 1.0, 0.0).astype(_BF16)
    dcum = carry_ref[0:1, :] + _dot_exact_rhs(tri, logf)
    carry_ref[0:1, :] = dcum[tt - 1:tt, :]
    dcum2 = dcum * LOG2E

    for p in range(N_PAIRS):
        cols = slice(p * PAIR, (p + 1) * PAIR)
        q2 = q_ref[:, cols]
        k2 = k_ref[:, cols]
        v2 = v_ref[:, cols]
        qn = q2 * lax.rsqrt(_half_sum(q2 * q2, lo_half) * (1.0 / HEAD_DIM) + NORM_EPS) * qg_ref[...]
        qn = qn * (HEAD_DIM ** -0.5 * LOG2E)
        kn = k2 * lax.rsqrt(_half_sum(k2 * k2, lo_half) * (1.0 / HEAD_DIM) + NORM_EPS) * kg_ref[...]
        for hh in range(2):
            h = 2 * p + hh
            d1, d2, d3 = _split3(dcum2[:, F_LANE0 + h:F_LANE0 + h + 1])
            base = HEAD_DIM if hh == 0 else 0
            keep = lo_half if hh == 0 else jnp.logical_not(lo_half)
            aq = jnp.where(lane < base + 3, 1.0,
                           jnp.where(lane == base + 3, d1,
                                     jnp.where(lane == base + 4, d2,
                                               jnp.where(lane == base + 5, d3, 0.0))))
            ak = jnp.where(lane == base, -d1,
                           jnp.where(lane == base + 1, -d2,
                                     jnp.where(lane == base + 2, -d3,
                                               jnp.where(lane < base + 6, 1.0, 0.0))))
            av = jnp.where(lane == base, 1.0, 0.0)
            qa_ref[0, h] = jnp.where(keep, qn, aq).astype(_BF16)
            ka_ref[0, h] = jnp.where(keep, kn, ak).astype(_BF16)
            vt_ref[0, h] = jnp.where(keep, v2, av).T.astype(_BF16)


def _fox_prep(pmat, qg2, kg2, bf_slot, batch, seq):
    tt = 256
    nt = seq // tt
    row_map = lambda c: (lambda b, t: (b * nt + t, c))
    row_sd = jax.ShapeDtypeStruct((batch, N_HEADS, seq, PAIR), _BF16)
    col_sd = jax.ShapeDtypeStruct((batch, N_HEADS, PAIR, seq), _BF16)
    row_spec = pl.BlockSpec((1, N_HEADS, tt, PAIR), lambda b, t: (b, 0, t, 0))
    col_spec = pl.BlockSpec((1, N_HEADS, PAIR, tt), lambda b, t: (b, 0, 0, t))
    vec = pl.BlockSpec((1, PAIR), lambda b, t: (0, 0))
    return pl.pallas_call(
        functools.partial(_fox_prep_kernel, tt=tt),
        grid=(batch, nt),
        in_specs=[pl.BlockSpec((tt, D_HEADS), row_map(0)),
                  pl.BlockSpec((tt, D_HEADS), row_map(1)),
                  pl.BlockSpec((tt, D_HEADS), row_map(2)),
                  pl.BlockSpec((tt, SMALL_W), row_map(6 * D_HEADS // SMALL_W)),
                  vec, vec, vec],
        out_specs=[row_spec, row_spec, col_spec, pl.BlockSpec((tt, SMALL_W), row_map(0))],
        out_shape=[row_sd, row_sd, col_sd, jax.ShapeDtypeStruct((batch * seq, SMALL_W), _BF16)],
        scratch_shapes=[pltpu.VMEM((8, PAIR), _F32)],
        compiler_params=_cparams(("parallel", "arbitrary"), 40),
        name="fox_prep",
    )(pmat, pmat, pmat, pmat, qg2, kg2, bf_slot)


def _fox_attn_kernel(q_ref, k_ref, vt_ref, g_ref, o_ref, m_ref, acc_ref, s_ref, *, tq):
    qi = pl.program_id(2)
    key = lax.broadcasted_iota(jnp.int32, (tq, tq), 0)
    qry = lax.broadcasted_iota(jnp.int32, (tq, tq), 1)
    m_ref[...] = jnp.full(m_ref.shape, NEG_INF, _F32)
    acc_ref[...] = jnp.zeros(acc_ref.shape, _F32)

    def scores(blk, slot, hh):
        k = k_ref[0, hh, pl.ds(pl.multiple_of(blk * tq, tq), tq), :]
        s_ref[slot, hh] = lax.dot_general(k, q_ref[0, hh], (((1,), (1,)), ((), ())),
                                          preferred_element_type=_F32)

    def absorb(blk, slot, hh, causal):
        vt = vt_ref[0, hh, :, pl.ds(pl.multiple_of(blk * tq, tq), tq)]
        st = s_ref[slot, hh]
        if causal:
            st = jnp.where(key <= qry, st, NEG_INF)
        m_old = m_ref[hh]
        m_new = jnp.maximum(m_old, jnp.max(st, axis=0, keepdims=True))
        p = jnp.exp2(st - m_new)
        acc_ref[hh] = jnp.exp2(m_old - m_new) * acc_ref[hh] + jnp.dot(
            vt, p.astype(_BF16), preferred_element_type=_F32)
        m_ref[hh] = m_new

    for hh in range(2):
        scores(0, 0, hh)

    def two_blocks(j2, carry):
        blk = 2 * j2
        for hh in range(2):
            scores(blk + 1, 1, hh)
            absorb(blk, 0, hh, False)
        for hh in range(2):
            scores(blk + 2, 0, hh)
            absorb(blk + 1, 1, hh, False)
        return carry

    lax.fori_loop(0, lax.shift_right_logical(qi, 1), two_blocks, 0)

    @pl.when((qi & 1) == 1)
    def _():
        for hh in range(2):
            scores(qi, 1, hh)
            absorb(qi - 1, 0, hh, False)
        for hh in range(2):
            absorb(qi, 1, hh, True)

    @pl.when((qi & 1) == 0)
    def _():
        for hh in range(2):
            absorb(qi, 0, hh, True)

    acc0, acc1 = acc_ref[0], acc_ref[1]
    o0 = acc0[:HEAD_DIM] / acc0[HEAD_DIM:HEAD_DIM + 1]
    o1 = acc1[HEAD_DIM:] / acc1[0:1]
    n0 = o0 * lax.rsqrt(jnp.mean(o0 * o0, axis=0, keepdims=True) + NORM_EPS)
    n1 = o1 * lax.rsqrt(jnp.mean(o1 * o1, axis=0, keepdims=True) + NORM_EPS)
    o_ref[0] = (jnp.concatenate([n0, n1], axis=0).T * g_ref[...]).astype(_BF16)


def _fox_attn(qa, ka, vt, og, batch, seq):
    tq = 512
    nq = seq // tq
    return pl.pallas_call(
        functools.partial(_fox_attn_kernel, tq=tq),
        grid=(batch, N_PAIRS, nq),
        in_specs=[pl.BlockSpec((1, 2, tq, PAIR), lambda b, p, i: (b, p, i, 0)),
                  pl.BlockSpec((1, 2, seq, PAIR), lambda b, p, i: (b, p, 0, 0)),
                  pl.BlockSpec((1, 2, PAIR, seq), lambda b, p, i: (b, p, 0, 0)),
                  pl.BlockSpec((1, PAIR), lambda b, p, i: (0, p))],
        out_specs=pl.BlockSpec((1, tq, PAIR), lambda b, p, i: (b, i, p)),
        out_shape=jax.ShapeDtypeStruct((batch, seq, D_HEADS), _BF16),
        scratch_shapes=[pltpu.VMEM((2, 1, tq), _F32), pltpu.VMEM((2, PAIR, tq), _F32),
                        pltpu.VMEM((2, 2, tq, tq), _F32)],
        compiler_params=_cparams(("parallel", "parallel", "arbitrary"), 40),
        name="fox_attn",
    )(qa, ka, vt, og)


def _stack_heads(x, lo_half):
    return jnp.concatenate([jnp.where(lo_half, x, 0.0), jnp.where(lo_half, 0.0, x)], axis=0)


def _interleave(*gens):
    live = list(gens)
    while live:
        for gen in list(live):
            try:
                next(gen)
            except StopIteration:
                live.remove(gen)


def _rwkv_kernel(r_ref, k_ref, v_ref, lo_ref, w0_ref, a0_ref, kk_ref, ka_ref, rk_ref, lg_ref, lb_ref,
                 db_ref, ab_ref, gb_ref, o_ref, z_ref, lhs_ref, add_ref, dec_ref, bv_ref, gate_ref,
                 *, tt, nt):
    nc = tt // CHUNK
    c2 = 2 * CHUNK
    t = pl.program_id(2)
    wr = lax.rem(t, 2)
    rd = 1 - wr
    lo_half = _lo_half_mask()

    @pl.when(t == 0)
    def _():
        z_ref[...] = jnp.zeros_like(z_ref)
        lhs_ref[1] = jnp.zeros(lhs_ref.shape[1:], lhs_ref.dtype)
        add_ref[1] = jnp.zeros(add_ref.shape[1:], add_ref.dtype)
        dec_ref[1] = jnp.ones(dec_ref.shape[1:], dec_ref.dtype)
        bv_ref[1] = jnp.zeros(bv_ref.shape[1:], bv_ref.dtype)
        gate_ref[1] = jnp.zeros(gate_ref.shape[1:], gate_ref.dtype)

    def advance():
        z = z_ref[...]
        dec = dec_ref[rd]
        ys = []
        for c in range(nc):
            res = jnp.dot(lhs_ref[rd, c], z.astype(_BF16), preferred_element_type=_F32)
            add = add_ref[rd, c]
            ysh = res[:c2] + add[:c2]
            ys.append(ysh[:CHUNK] + ysh[CHUNK:])
            z = dec[:, c:c + 1] * z + res[c2:] + add[c2:]
            yield
        z_ref[...] = z
        y = jnp.concatenate(ys, axis=0)
        mu = _half_sum(y, lo_half) * (1.0 / HEAD_DIM)
        dlt = y - mu
        var = _half_sum(dlt * dlt, lo_half) * (1.0 / HEAD_DIM)
        yn = dlt * lax.rsqrt(var + LNX_EPS) * lg_ref[...] + lb_ref[...]
        o_ref[0] = ((yn + bv_ref[rd]) * gate_ref[rd]).astype(_BF16)
        yield

    def prepare():
        r = r_ref[...]
        k = k_ref[...]
        v = v_ref[...]
        w = -_softplus(-(w0_ref[...] + jnp.dot(lo_ref[:, 0:PAIR], db_ref[...],
                                                 preferred_element_type=_F32))) - 0.5
        logd = -jnp.exp(w)
        a = _sigmoid(a0_ref[...] + jnp.dot(lo_ref[:, PAIR:2 * PAIR], ab_ref[...],
                                           preferred_element_type=_F32))
        gate_ref[wr] = jnp.dot(lo_ref[:, 2 * PAIR:], gb_ref[...], preferred_element_type=_F32)
        kk = k * kk_ref[...]
        kk = kk * lax.rsqrt(jnp.maximum(_half_sum(kk * kk, lo_half), 1e-24))
        k = k * (1.0 + (a - 1.0) * ka_ref[...])
        bb = kk * a
        bv_ref[wr] = _half_sum(r * k * rk_ref[...], lo_half) * v
        yield

        pos = lax.broadcasted_iota(jnp.int32, (tt, 1), 0) & (CHUNK - 1)
        lcum = logd
        for j in range(CHUNK_LOG2):
            lcum = lcum + jnp.where(pos >= (1 << j), pltpu.roll(lcum, 1 << j, axis=0), 0.0)
        yield

        r2 = lax.broadcasted_iota(jnp.int32, (c2, c2), 0)
        q2 = lax.broadcasted_iota(jnp.int32, (c2, c2), 1)
        same = (r2 >> CHUNK_LOG2) == (q2 >> CHUNK_LOG2)
        strict = same & (r2 > q2)
        incl = same & (r2 >= q2)
        eye = jnp.where(r2 == q2, 1.0, 0.0)
        zeros_cc = jnp.zeros((c2, PAIR), _F32)

        rng = range(nc)
        sl = [slice(c * CHUNK, (c + 1) * CHUNK) for c in rng]
        lc = [lcum[sl[c]] for c in rng]
        lc_rows = [lc[c][CHUNK - 1:CHUNK, :] for c in rng]
        dec_ref[wr] = jnp.exp(jnp.concatenate(lc_rows + [jnp.zeros((PAIR - nc, PAIR), _F32)], axis=0).T)
        e_pos = [jnp.exp(lc[c]) for c in rng]
        e_neg = [jnp.exp(-lc[c]) for c in rng]
        e_end = [jnp.exp(lc_rows[c] - lc[c]) for c in rng]
        rp = [_stack_heads(r[sl[c]] * e_pos[c], lo_half) for c in rng]
        ap = [_stack_heads(-kk[sl[c]] * jnp.exp(lc[c] - logd[sl[c]]), lo_half) for c in rng]
        kq = [_stack_heads(k[sl[c]] * e_neg[c], lo_half) for c in rng]
        bq = [_stack_heads(bb[sl[c]] * e_neg[c], lo_half) for c in rng]
        kc = [_stack_heads(k[sl[c]] * e_end[c], lo_half) for c in rng]
        bc = [_stack_heads(bb[sl[c]] * e_end[c], lo_half) for c in rng]
        vs = [_stack_heads(v[sl[c]], lo_half) for c in rng]
        yield

        s = [_dot_nt(jnp.concatenate([ap[c], rp[c]], axis=0), jnp.concatenate([bq[c], kq[c]], axis=0))
             for c in rng]
        a_ab = [jnp.where(strict, s[c][:c2, :c2], 0.0) for c in rng]
        a_ak = [jnp.where(strict, s[c][:c2, c2:], 0.0) for c in rng]
        a_rb = [jnp.where(incl, s[c][c2:, :c2], 0.0) for c in rng]
        a_rk = [jnp.where(incl, s[c][c2:, c2:], 0.0) for c in rng]
        yield
        akv = [_dot(a_ak[c], vs[c]) for c in rng]

        pw = [_dot(a_ab[c], a_ab[c]) for c in rng]
        tinv = [eye + a_ab[c] for c in rng]
        yield
        for _ in range(CHUNK_LOG2 - 2):
            pt = [_dot(pw[c], jnp.concatenate([pw[c], tinv[c]], axis=1)) for c in rng]
            pw = [pt[c][:, :c2] for c in rng]
            tinv = [tinv[c] + pt[c][:, c2:] for c in rng]
            yield
        q5 = [_dot(tinv[c], jnp.concatenate([ap[c], akv[c]], axis=1)) for c in rng]
        yield
        wu = [q5[c] + _dot(pw[c], q5[c]) for c in rng]
        yield
        x = [jnp.concatenate([wu[c], jnp.concatenate([zeros_cc, vs[c]], axis=1)], axis=0) for c in rng]
        ry = [_dot(jnp.concatenate([a_rb[c], a_rk[c]], axis=1), x[c]) for c in rng]
        mz = [_dot_tn(jnp.concatenate([bc[c], kc[c]], axis=0), x[c]) for c in rng]
        yield
        for c in rng:
            lhs_ref[wr, c] = jnp.concatenate([rp[c] + ry[c][:, :PAIR], mz[c][:, :PAIR]], axis=0).astype(_BF16)
            add_ref[wr, c] = jnp.concatenate([ry[c][:, PAIR:], mz[c][:, PAIR:]], axis=0)
        yield

    @pl.when(t < nt)
    def _():
        _interleave(prepare(), advance())

    @pl.when(t == nt)
    def _():
        _interleave(advance())


def _rwkv(pmat, lora_in, prm, batch, seq):
    tt = 512
    nt = seq // tt
    nc = tt // CHUNK
    cur = lambda b, t: b * nt + jnp.minimum(t, nt - 1)
    blk = lambda c0: pl.BlockSpec((tt, PAIR), lambda b, p, t, c0=c0: (cur(b, t), c0 + p))
    vec = pl.BlockSpec((1, PAIR), lambda b, p, t: (0, p))
    lora = lambda rows: pl.BlockSpec((rows, PAIR), lambda b, p, t: (0, p))
    c_r = 3 * D_HEADS // PAIR
    return pl.pallas_call(
        functools.partial(_rwkv_kernel, tt=tt, nt=nt),
        grid=(batch, N_PAIRS, nt + 1),
        in_specs=[blk(c_r), blk(c_r + N_PAIRS), blk(c_r + 2 * N_PAIRS),
                  pl.BlockSpec((tt, SMALL_W), lambda b, p, t: (cur(b, t), 0)),
                  vec, vec, vec, vec, vec, vec, vec,
                  lora(PAIR), lora(PAIR), lora(GATE_LORA)],
        out_specs=pl.BlockSpec((1, tt, PAIR), lambda b, p, t: (b, jnp.maximum(t - 1, 0), p)),
        out_shape=jax.ShapeDtypeStruct((batch, seq, D_HEADS), _BF16),
        scratch_shapes=[pltpu.VMEM((PAIR, PAIR), _F32),
                        pltpu.VMEM((2, nc, 2 * PAIR, PAIR), _BF16),
                        pltpu.VMEM((2, nc, 2 * PAIR, PAIR), _F32),
                        pltpu.VMEM((2, PAIR, PAIR), _F32),
                        pltpu.VMEM((2, tt, PAIR), _F32),
                        pltpu.VMEM((2, tt, PAIR), _F32)],
        compiler_params=_cparams(("parallel", "parallel", "arbitrary"), 48),
        name="rwkv",
    )(pmat, pmat, pmat, lora_in, prm["w0"], prm["a0"], prm["k_k"], prm["k_a"], prm["r_k"],
      prm["lnx_g"], prm["lnx_b"], prm["decay_b"], prm["a_b"], prm["g_b"])


def _outproj_kernel(fox_ref, rw_ref, x_ref, w1_ref, w2_ref, gt_ref, g2_ref, sh_ref, sc_ref,
                    x1_ref, h2_ref):
    y = (jnp.dot(fox_ref[...], w1_ref[...], preferred_element_type=_F32)
         + jnp.dot(rw_ref[...], w2_ref[...], preferred_element_type=_F32))
    x1 = x_ref[...] + gt_ref[0] * y
    x1_ref[...] = x1
    h = x1 * lax.rsqrt(jnp.mean(x1 * x1, axis=-1, keepdims=True) + NORM_EPS) * g2_ref[...]
    h2_ref[...] = (h * (1.0 + sc_ref[0]) + sh_ref[0]).astype(_BF16)


def _outproj(fox, rw, x2, w1, w2, gt, g2, sh, sc, seq):
    m, d = x2.shape
    tm = 256
    per_b = seq // tm
    rows = lambda w: pl.BlockSpec((tm, w), lambda i: (i, 0))
    full = lambda a: pl.BlockSpec(a.shape, lambda i: (0, 0))
    mod = pl.BlockSpec((1, 1, d), lambda i: (i // per_b, 0, 0))
    return pl.pallas_call(
        _outproj_kernel,
        grid=(m // tm,),
        in_specs=[rows(D_HEADS), rows(D_HEADS), rows(d), full(w1), full(w2), mod, full(g2), mod, mod],
        out_specs=[rows(d), rows(d)],
        out_shape=[jax.ShapeDtypeStruct((m, d), _F32), jax.ShapeDtypeStruct((m, d), _BF16)],
        compiler_params=_cparams(("parallel",), 48),
        name="outproj",
    )(fox, rw, x2, w1, w2, gt, g2, sh, sc)


def _mlp_kernel(h_ref, wu_ref, wd_ref, x1_ref, gt_ref, o_ref, acc_ref):
    f = pl.program_id(1)

    @pl.when(f == 0)
    def _():
        acc_ref[...] = jnp.zeros_like(acc_ref)

    u = jnp.dot(h_ref[...], wu_ref[...], preferred_element_type=_F32)
    act = jnp.square(jnp.maximum(u, 0.0)).astype(_BF16)
    acc_ref[...] += jnp.dot(act, wd_ref[...], preferred_element_type=_F32)

    @pl.when(f == pl.num_programs(1) - 1)
    def _():
        o_ref[...] = x1_ref[...] + gt_ref[0] * acc_ref[...]


def _mlp(h2, wu, wd, x1, gt, seq):
    m, d = x1.shape
    ff = wu.shape[1]
    tm, tf = 1024, 1024
    per_b = seq // tm
    rows = lambda: pl.BlockSpec((tm, d), lambda i, f: (i, 0), pipeline_mode=pl.Buffered(1))
    return pl.pallas_call(
        _mlp_kernel,
        grid=(m // tm, ff // tf),
        in_specs=[rows(),
                  pl.BlockSpec((d, tf), lambda i, f: (0, f)),
                  pl.BlockSpec((tf, d), lambda i, f: (f, 0)),
                  rows(),
                  pl.BlockSpec((1, 1, d), lambda i, f: (i // per_b, 0, 0))],
        out_specs=rows(),
        out_shape=jax.ShapeDtypeStruct((m, d), _F32),
        scratch_shapes=[pltpu.VMEM((tm, d), _F32)],
        compiler_params=_cparams(("parallel", "arbitrary"), 56),
        name="mlp",
    )(h2, wu, wd, x1, gt)


def _pad_rows(a, rows):
    return jnp.pad(a, ((0, rows - a.shape[0]), (0, 0)))


def _pad_lanes(a, width):
    return jnp.pad(a, ((0, 0), (0, width - a.shape[1])))


def _pack_kernel(w_ref, o_ref):
    o_f = 3 * D_HEADS
    o_r = o_f + N_HEADS
    o_w = o_r + 3 * D_HEADS
    o_a = o_w + DECAY_LORA
    o_g = o_a + ICLR_LORA
    rows = w_ref.shape[0]
    zeros = lambda n: jnp.zeros((rows, n), _BF16)
    o_ref[:, 0:o_f] = w_ref[:, 0:o_f].astype(_BF16)
    o_ref[:, o_f:2 * o_f] = w_ref[:, o_r:o_w].astype(_BF16)
    o_ref[:, 2 * o_f:] = jnp.concatenate(
        [w_ref[:, o_w:o_a].astype(_BF16), w_ref[:, o_f:o_r].astype(_BF16), zeros(PAIR - DECAY_LORA - N_HEADS),
         w_ref[:, o_a:o_g].astype(_BF16), zeros(PAIR - ICLR_LORA), w_ref[:, o_g:].astype(_BF16)], axis=1)


def _pack_w_in(w):
    d, n = w.shape
    tr = 256
    return pl.pallas_call(
        _pack_kernel,
        grid=(d // tr,),
        in_specs=[pl.BlockSpec((tr, n), lambda i: (i, 0))],
        out_specs=pl.BlockSpec((tr, 6 * D_HEADS + SMALL_W), lambda i: (i, 0)),
        out_shape=jax.ShapeDtypeStruct((d, 6 * D_HEADS + SMALL_W), _BF16),
        compiler_params=_cparams(("parallel",), 40),
        name="pack_w_in",
    )(w)


def _pack_shift_mu(mu):
    o_w = 3 * D_HEADS
    o_a = o_w + DECAY_LORA
    o_g = o_a + ICLR_LORA
    return jnp.concatenate([jnp.zeros((1, 3 * D_HEADS), _F32), mu[:, :o_w], _pad_lanes(mu[:, o_w:o_a], PAIR),
                            _pad_lanes(mu[:, o_a:o_g], PAIR), mu[:, o_g:]], axis=1)


def kernel(x, c, w_ada, b_ada, norm1_g, w_in, b_forget, q_norm_g, k_norm_g, fox_out_g, shift_mu, w0, decay_b, a0, a_b, g_b, k_k, k_a, r_k, lnx_g, lnx_b, w_out, norm2_g, w_mlp_up, w_mlp_down):
    batch, seq, d = x.shape
    depth = w_ada.shape[0]
    x2 = x.reshape(batch * seq, d)
    row = lambda a: a.reshape(1, -1)
    for l in range(depth):
        mod = _ada(c, w_ada[l], b_ada[l])
        sh1, sc1, gt1, sh2, sc2, gt2 = [m.reshape(batch, 1, d) for m in jnp.split(mod, 6, axis=-1)]

        pmat = _proj(x2, row(norm1_g[l]), sh1, sc1, _pack_w_in(w_in[l]), _pack_shift_mu(row(shift_mu[l])), seq)

        bf_slot = jnp.pad(row(b_forget[l]), ((0, 0), (F_LANE0, PAIR - F_LANE0 - N_HEADS)))
        qa, ka, vt, lora_in = _fox_prep(pmat, jnp.tile(row(q_norm_g[l]), (1, 2)),
                                        jnp.tile(row(k_norm_g[l]), (1, 2)), bf_slot, batch, seq)
        fox = _fox_attn(qa, ka, vt, row(fox_out_g[l]), batch, seq)

        prm = {
            "w0": row(w0[l]), "a0": row(a0[l]), "k_k": row(k_k[l]), "k_a": row(k_a[l]),
            "r_k": row(r_k[l]), "lnx_g": row(lnx_g[l]), "lnx_b": row(lnx_b[l]),
            "decay_b": _pad_rows(decay_b[l], PAIR).astype(_BF16),
            "a_b": _pad_rows(a_b[l], PAIR).astype(_BF16),
            "g_b": g_b[l].astype(_BF16),
        }
        rw = _rwkv(pmat, lora_in, prm, batch, seq)

        w_o = w_out[l].astype(_BF16)
        x1, h2 = _outproj(fox.reshape(batch * seq, D_HEADS), rw.reshape(batch * seq, D_HEADS), x2,
                          w_o[:D_HEADS], w_o[D_HEADS:], gt1, row(norm2_g[l]), sh2, sc2, seq)
        x2 = _mlp(h2, w_mlp_up[l].astype(_BF16), w_mlp_down[l].astype(_BF16), x1, gt2, seq)
    return x2.reshape(batch, seq, d)
```

```python
import functools

import jax
import jax.numpy as jnp
from jax import lax
from jax.experimental import pallas as pl
from jax.experimental.pallas import tpu as pltpu

_F32 = jnp.float32
_BF16 = jnp.bfloat16

HEAD_DIM = 64
PAIR = 2 * HEAD_DIM
N_HEADS = 16
D_HEADS = N_HEADS * HEAD_DIM
N_PAIRS = N_HEADS // 2
DECAY_LORA = 96
ICLR_LORA = 96
GATE_LORA = 256
SMALL_W = 512
F_LANE0 = DECAY_LORA
NORM_EPS = 1e-6
LNX_EPS = 64e-5
NEG_INF = -1e30
LOG2E = 1.4426950408889634
CHUNK_LOG2 = 6
CHUNK = 1 << CHUNK_LOG2
MIB = 1024 * 1024


def _cparams(sem, vmem_mib):
    return pltpu.CompilerParams(dimension_semantics=sem, vmem_limit_bytes=vmem_mib * MIB)


def _dot(a, b):
    return jnp.dot(a.astype(_BF16), b.astype(_BF16), preferred_element_type=_F32)


def _dot_nt(a, b):
    return lax.dot_general(a.astype(_BF16), b.astype(_BF16), (((1,), (1,)), ((), ())),
                           preferred_element_type=_F32)


def _dot_tn(a, b):
    return lax.dot_general(a.astype(_BF16), b.astype(_BF16), (((0,), (0,)), ((), ())),
                           preferred_element_type=_F32)


def _split3(x):
    h = x.astype(_BF16).astype(_F32)
    r = x - h
    m = r.astype(_BF16).astype(_F32)
    l = (r - m).astype(_BF16).astype(_F32)
    return h, m, l


def _dot_exact_rhs(a_bf16, x):
    return sum(jnp.dot(a_bf16, p.astype(_BF16), preferred_element_type=_F32) for p in _split3(x))


def _sigmoid(x):
    return 1.0 / (1.0 + jnp.exp(-x))


def _softplus(x):
    return jnp.maximum(x, 0.0) + jnp.log(1.0 + jnp.exp(-jnp.abs(x)))


def _half_sum(x, lo_half):
    s_lo = jnp.sum(jnp.where(lo_half, x, 0.0), axis=-1, keepdims=True)
    s_hi = jnp.sum(jnp.where(lo_half, 0.0, x), axis=-1, keepdims=True)
    return jnp.where(lo_half, s_lo, s_hi)


def _lo_half_mask():
    return lax.broadcasted_iota(jnp.int32, (1, PAIR), 1) < HEAD_DIM


def _ada_kernel(c_ref, w_ref, b_ref, o_ref):
    c = c_ref[...]
    cond = c * _sigmoid(c)
    o_ref[...] = _dot(cond, w_ref[...]) + b_ref[...]


def _ada(c, w_ada, b_ada):
    b, d = c.shape
    n = w_ada.shape[1]
    rows = 8 * pl.cdiv(b, 8)
    tn = 1024
    cp = jnp.pad(c, ((0, rows - b), (0, 0)))
    out = pl.pallas_call(
        _ada_kernel,
        grid=(n // tn,),
        in_specs=[pl.BlockSpec((rows, d), lambda j: (0, 0)),
                  pl.BlockSpec((d, tn), lambda j: (0, j)),
                  pl.BlockSpec((1, tn), lambda j: (0, j))],
        out_specs=pl.BlockSpec((rows, tn), lambda j: (0, j)),
        out_shape=jax.ShapeDtypeStruct((rows, n), _F32),
        compiler_params=_cparams(("parallel",), 40),
        name="ada",
    )(cp, w_ada, b_ada.reshape(1, n))
    return out[:b]


def _proj_kernel(x_ref, g_ref, sh_ref, sc_ref, w_ref, mu_ref, o_ref, h_ref, carry_ref, *, tm, per_b):
    i = pl.program_id(0)
    j = pl.program_id(1)

    @pl.when(j == 0)
    def _():
        x = x_ref[...]
        y = x * lax.rsqrt(jnp.mean(x * x, axis=-1, keepdims=True) + NORM_EPS) * g_ref[...]
        h_ref[...] = (y * (1.0 + sc_ref[0]) + sh_ref[0]).astype(_BF16)

    y = lax.dot_general(h_ref[...], w_ref[...], (((1,), (1,)), ((), ())), preferred_element_type=_F32)
    first_row = lax.broadcasted_iota(jnp.int32, (tm, 1), 0) == 0
    above = jnp.where(lax.rem(i, per_b) == 0, 0.0, carry_ref[j, 0:1, :])
    prev = jnp.where(first_row, above, pltpu.roll(y, 1, axis=0))
    carry_ref[j, 0:1, :] = y[tm - 1:tm, :]
    o_ref[...] = y + (prev - y) * mu_ref[...]


def _proj(x2, g, sh, sc, wt, mu, seq):
    m, d = x2.shape
    n = wt.shape[0]
    tm, tn = 1024, 512
    per_b = seq // tm
    return pl.pallas_call(
        functools.partial(_proj_kernel, tm=tm, per_b=per_b),
        grid=(m // tm, n // tn),
        in_specs=[pl.BlockSpec((tm, d), lambda i, j: (i, 0)),
                  pl.BlockSpec((1, d), lambda i, j: (0, 0)),
                  pl.BlockSpec((1, 1, d), lambda i, j: (i // per_b, 0, 0)),
                  pl.BlockSpec((1, 1, d), lambda i, j: (i // per_b, 0, 0)),
                  pl.BlockSpec((tn, d), lambda i, j: (j, 0)),
                  pl.BlockSpec((1, tn), lambda i, j: (0, j))],
        out_specs=pl.BlockSpec((tm, tn), lambda i, j: (i, j)),
        out_shape=jax.ShapeDtypeStruct((m, n), _F32),
        scratch_shapes=[pltpu.VMEM((tm, d), _BF16), pltpu.VMEM((n // tn, 8, tn), _F32)],
        compiler_params=_cparams(("arbitrary", "arbitrary"), 48),
        name="proj",
    )(x2, g, sh, sc, wt, mu)


def _fox_prep_kernel(q_ref, k_ref, v_ref, s_ref, qg_ref, kg_ref, bf_ref, qt_ref, ka_ref, vt_ref,
                     lo_ref, carry_ref, *, tt):
    @pl.when(pl.program_id(1) == 0)
    def _():
        carry_ref[...] = jnp.zeros_like(carry_ref)

    lo_ref[:, 0:PAIR] = jnp.tanh(s_ref[:, 0:PAIR]).astype(_BF16)
    lo_ref[:, PAIR:2 * PAIR] = s_ref[:, PAIR:2 * PAIR].astype(_BF16)
    lo_ref[:, 2 * PAIR:] = _sigmoid(s_ref[:, 2 * PAIR:]).astype(_BF16)

    z = s_ref[:, 0:PAIR] + bf_ref[...]
    logf = -_softplus(-z)
    row = lax.broadcasted_iota(jnp.int32, (tt, tt), 0)
    col = lax.broadcasted_iota(jnp.int32, (tt, tt), 1)
    tri = jnp.where(row >= col, 1.0, 0.0).astype(_BF16)
    dcum = carry_ref[0:1, :] + _dot_exact_rhs(tri, logf)
    carry_ref[0:1, :] = dcum[tt - 1:tt, :]
    d1, d2, d3 = _split3((dcum * LOG2E).T[F_LANE0:F_LANE0 + N_HEADS, :])

    r8 = lax.broadcasted_iota(jnp.int32, (8, tt), 0)
    lo_rows = lax.broadcasted_iota(jnp.int32, (PAIR, 1), 0) < HEAD_DIM
    ones_row = jnp.where(r8 == 0, 1.0, 0.0)
    fill = jnp.zeros((HEAD_DIM - 8, tt), _F32)
    gq = jnp.broadcast_to(qg_ref[...] * (HEAD_DIM ** -0.5 * LOG2E), (PAIR, tt))
    gk = jnp.broadcast_to(kg_ref[...], (PAIR, tt))

    def normed(x_t, gain):
        sq = x_t * x_t
        ms = jnp.where(lo_rows, jnp.sum(sq[:HEAD_DIM], axis=0, keepdims=True),
                       jnp.sum(sq[HEAD_DIM:], axis=0, keepdims=True)) * (1.0 / HEAD_DIM)
        return x_t * lax.rsqrt(ms + NORM_EPS) * gain

    for p in range(N_PAIRS):
        cols = slice(p * PAIR, (p + 1) * PAIR)
        qn = normed(q_ref[:, cols].T, gq)
        kn = normed(k_ref[:, cols].T, gk)
        vt = v_ref[:, cols].T
        for hh in range(2):
            h = 2 * p + hh
            a1, a2, a3 = d1[h:h + 1], d2[h:h + 1], d3[h:h + 1]
            aq = jnp.where(r8 < 3, 1.0, jnp.where(r8 == 3, a1, jnp.where(r8 == 4, a2, jnp.where(r8 == 5, a3, 0.0))))
            ak = jnp.where(r8 == 0, -a1, jnp.where(r8 == 1, -a2, jnp.where(r8 == 2, -a3, jnp.where(r8 < 6, 1.0, 0.0))))

            def with_rows(x_t, extra, hh=hh):
                if hh == 0:
                    return jnp.concatenate([x_t[:HEAD_DIM], extra, fill], axis=0)
                return jnp.concatenate([extra, fill, x_t[HEAD_DIM:]], axis=0)

            qt_ref[0, h] = with_rows(qn, aq).astype(_BF16)
            ka_ref[0, h] = with_rows(kn, ak).T.astype(_BF16)
            vt_ref[0, h] = with_rows(vt, ones_row).astype(_BF16)


def _fox_prep(pmat, qg2, kg2, bf_slot, batch, seq):
    tt = 256
    nt = seq // tt
    row_map = lambda c: (lambda b, t: (b * nt + t, c))
    row_sd = jax.ShapeDtypeStruct((batch, N_HEADS, seq, PAIR), _BF16)
    col_sd = jax.ShapeDtypeStruct((batch, N_HEADS, PAIR, seq), _BF16)
    row_spec = pl.BlockSpec((1, N_HEADS, tt, PAIR), lambda b, t: (b, 0, t, 0))
    col_spec = pl.BlockSpec((1, N_HEADS, PAIR, tt), lambda b, t: (b, 0, 0, t))
    vec = pl.BlockSpec((1, PAIR), lambda b, t: (0, 0))
    colvec = pl.BlockSpec((PAIR, 1), lambda b, t: (0, 0))
    return pl.pallas_call(
        functools.partial(_fox_prep_kernel, tt=tt),
        grid=(batch, nt),
        in_specs=[pl.BlockSpec((tt, D_HEADS), row_map(0)),
                  pl.BlockSpec((tt, D_HEADS), row_map(1)),
                  pl.BlockSpec((tt, D_HEADS), row_map(2)),
                  pl.BlockSpec((tt, SMALL_W), row_map(6 * D_HEADS // SMALL_W)),
                  colvec, colvec, vec],
        out_specs=[col_spec, row_spec, col_spec, pl.BlockSpec((tt, SMALL_W), row_map(0))],
        out_shape=[col_sd, row_sd, col_sd, jax.ShapeDtypeStruct((batch * seq, SMALL_W), _BF16)],
        scratch_shapes=[pltpu.VMEM((8, PAIR), _F32)],
        compiler_params=_cparams(("parallel", "arbitrary"), 40),
        name="fox_prep",
    )(pmat, pmat, pmat, pmat, qg2, kg2, bf_slot)


def _fox_attn_kernel(qt_ref, k_ref, vt_ref, g_ref, o_ref, m_ref, acc_ref, s_ref, *, tq):
    qi = pl.program_id(2)
    key = lax.broadcasted_iota(jnp.int32, (tq, tq), 0)
    qry = lax.broadcasted_iota(jnp.int32, (tq, tq), 1)
    m_ref[...] = jnp.full(m_ref.shape, NEG_INF, _F32)
    acc_ref[...] = jnp.zeros(acc_ref.shape, _F32)

    def scores(blk, slot, hh):
        k = k_ref[0, hh, pl.ds(pl.multiple_of(blk * tq, tq), tq), :]
        s_ref[slot, hh] = jnp.dot(k, qt_ref[0, hh], preferred_element_type=_F32)

    def absorb(blk, slot, hh, causal):
        vt = vt_ref[0, hh, :, pl.ds(pl.multiple_of(blk * tq, tq), tq)]
        st = s_ref[slot, hh]
        if causal:
            st = jnp.where(key <= qry, st, NEG_INF)
        m_old = m_ref[hh]
        m_new = jnp.maximum(m_old, jnp.max(st, axis=0, keepdims=True))
        p = jnp.exp2(st - m_new)
        acc_ref[hh] = jnp.exp2(m_old - m_new) * acc_ref[hh] + jnp.dot(
            vt, p.astype(_BF16), preferred_element_type=_F32)
        m_ref[hh] = m_new

    for hh in range(2):
        scores(0, 0, hh)

    def two_blocks(j2, carry):
        blk = 2 * j2
        for hh in range(2):
            scores(blk + 1, 1, hh)
            absorb(blk, 0, hh, False)
        for hh in range(2):
            scores(blk + 2, 0, hh)
            absorb(blk + 1, 1, hh, False)
        return carry

    lax.fori_loop(0, lax.shift_right_logical(qi, 1), two_blocks, 0)

    @pl.when((qi & 1) == 1)
    def _():
        for hh in range(2):
            scores(qi, 1, hh)
            absorb(qi - 1, 0, hh, False)
        for hh in range(2):
            absorb(qi, 1, hh, True)

    @pl.when((qi & 1) == 0)
    def _():
        for hh in range(2):
            absorb(qi, 0, hh, True)

    acc0, acc1 = acc_ref[0], acc_ref[1]
    o0 = acc0[:HEAD_DIM] / acc0[HEAD_DIM:HEAD_DIM + 1]
    o1 = acc1[HEAD_DIM:] / acc1[0:1]
    n0 = o0 * lax.rsqrt(jnp.mean(o0 * o0, axis=0, keepdims=True) + NORM_EPS)
    n1 = o1 * lax.rsqrt(jnp.mean(o1 * o1, axis=0, keepdims=True) + NORM_EPS)
    o_ref[0] = (jnp.concatenate([n0, n1], axis=0).T * g_ref[...]).astype(_BF16)


def _fox_attn(qt, ka, vt, og, batch, seq):
    tq = 512
    nq = seq // tq
    return pl.pallas_call(
        functools.partial(_fox_attn_kernel, tq=tq),
        grid=(batch, N_PAIRS, nq),
        in_specs=[pl.BlockSpec((1, 2, PAIR, tq), lambda b, p, i: (b, p, 0, i)),
                  pl.BlockSpec((1, 2, seq, PAIR), lambda b, p, i: (b, p, 0, 0)),
                  pl.BlockSpec((1, 2, PAIR, seq), lambda b, p, i: (b, p, 0, 0)),
                  pl.BlockSpec((1, PAIR), lambda b, p, i: (0, p))],
        out_specs=pl.BlockSpec((1, tq, PAIR), lambda b, p, i: (b, i, p)),
        out_shape=jax.ShapeDtypeStruct((batch, seq, D_HEADS), _BF16),
        scratch_shapes=[pltpu.VMEM((2, 1, tq), _F32), pltpu.VMEM((2, PAIR, tq), _F32),
                        pltpu.VMEM((2, 2, tq, tq), _F32)],
        compiler_params=_cparams(("parallel", "parallel", "arbitrary"), 40),
        name="fox_attn",
    )(qt, ka, vt, og)


def _stack_heads(x, lo_half):
    return jnp.concatenate([jnp.where(lo_half, x, 0.0), jnp.where(lo_half, 0.0, x)], axis=0)


def _interleave(*gens):
    live = list(gens)
    while live:
        for gen in list(live):
            try:
                next(gen)
            except StopIteration:
                live.remove(gen)


def _rwkv_kernel(r_ref, k_ref, v_ref, lo_ref, w0_ref, a0_ref, kk_ref, ka_ref, rk_ref, lg_ref, lb_ref,
                 db_ref, ab_ref, gb_ref, o_ref, z_ref, lhs_ref, add_ref, dec_ref, bv_ref, gate_ref,
                 *, tt, nt):
    nc = tt // CHUNK
    c2 = 2 * CHUNK
    t = pl.program_id(2)
    wr = lax.rem(t, 2)
    rd = 1 - wr
    lo_half = _lo_half_mask()

    @pl.when(t == 0)
    def _():
        z_ref[...] = jnp.zeros_like(z_ref)
        lhs_ref[1] = jnp.zeros(lhs_ref.shape[1:], lhs_ref.dtype)
        add_ref[1] = jnp.zeros(add_ref.shape[1:], add_ref.dtype)
        dec_ref[1] = jnp.ones(dec_ref.shape[1:], dec_ref.dtype)
        bv_ref[1] = jnp.zeros(bv_ref.shape[1:], bv_ref.dtype)
        gate_ref[1] = jnp.zeros(gate_ref.shape[1:], gate_ref.dtype)

    def advance():
        z = z_ref[...]
        dec = dec_ref[rd]
        ys = []
        for c in range(nc):
            res = jnp.dot(lhs_ref[rd, c], z.astype(_BF16), preferred_element_type=_F32)
            add = add_ref[rd, c]
            ysh = res[:c2] + add[:c2]
            ys.append(ysh[:CHUNK] + ysh[CHUNK:])
            z = dec[:, c:c + 1] * z + res[c2:] + add[c2:]
            yield
        z_ref[...] = z
        y = jnp.concatenate(ys, axis=0)
        mu = _half_sum(y, lo_half) * (1.0 / HEAD_DIM)
        dlt = y - mu
        var = _half_sum(dlt * dlt, lo_half) * (1.0 / HEAD_DIM)
        yn = dlt * lax.rsqrt(var + LNX_EPS) * lg_ref[...] + lb_ref[...]
        o_ref[0] = ((yn + bv_ref[rd]) * gate_ref[rd]).astype(_BF16)
        yield

    def prepare():
        r = r_ref[...]
        k = k_ref[...]
        v = v_ref[...]
        w = -_softplus(-(w0_ref[...] + jnp.dot(lo_ref[:, 0:PAIR], db_ref[...],
                                                 preferred_element_type=_F32))) - 0.5
        logd = -jnp.exp(w)
        a = _sigmoid(a0_ref[...] + jnp.dot(lo_ref[:, PAIR:2 * PAIR], ab_ref[...],
                                           preferred_element_type=_F32))
        gate_ref[wr] = jnp.dot(lo_ref[:, 2 * PAIR:], gb_ref[...], preferred_element_type=_F32)
        kk = k * kk_ref[...]
        kk = kk * lax.rsqrt(jnp.maximum(_half_sum(kk * kk, lo_half), 1e-24))
        k = k * (1.0 + (a - 1.0) * ka_ref[...])
        bb = kk * a
        bv_ref[wr] = _half_sum(r * k * rk_ref[...], lo_half) * v
        yield

        pos = lax.broadcasted_iota(jnp.int32, (tt, 1), 0) & (CHUNK - 1)
        lcum = logd
        for j in range(CHUNK_LOG2):
            lcum = lcum + jnp.where(pos >= (1 << j), pltpu.roll(lcum, 1 << j, axis=0), 0.0)
        yield

        r2 = lax.broadcasted_iota(jnp.int32, (c2, c2), 0)
        q2 = lax.broadcasted_iota(jnp.int32, (c2, c2), 1)
        same = (r2 >> CHUNK_LOG2) == (q2 >> CHUNK_LOG2)
        strict = same & (r2 > q2)
        incl = same & (r2 >= q2)
        eye = jnp.where(r2 == q2, 1.0, 0.0)
        zeros_cc = jnp.zeros((c2, PAIR), _F32)

        rng = range(nc)
        sl = [slice(c * CHUNK, (c + 1) * CHUNK) for c in rng]
        lc = [lcum[sl[c]] for c in rng]
        lc_rows = [lc[c][CHUNK - 1:CHUNK, :] for c in rng]
        dec_ref[wr] = jnp.exp(jnp.concatenate(lc_rows + [jnp.zeros((PAIR - nc, PAIR), _F32)], axis=0).T)
        e_pos = [jnp.exp(lc[c]) for c in rng]
        e_neg = [jnp.exp(-lc[c]) for c in rng]
        e_end = [jnp.exp(lc_rows[c] - lc[c]) for c in rng]
        rp = [_stack_heads(r[sl[c]] * e_pos[c], lo_half) for c in rng]
        ap = [_stack_heads(-kk[sl[c]] * jnp.exp(lc[c] - logd[sl[c]]), lo_half) for c in rng]
        kq = [_stack_heads(k[sl[c]] * e_neg[c], lo_half) for c in rng]
        bq = [_stack_heads(bb[sl[c]] * e_neg[c], lo_half) for c in rng]
        kc = [_stack_heads(k[sl[c]] * e_end[c], lo_half) for c in rng]
        bc = [_stack_heads(bb[sl[c]] * e_end[c], lo_half) for c in rng]
        vs = [_stack_heads(v[sl[c]], lo_half) for c in rng]
        yield

        s = [_dot_nt(jnp.concatenate([ap[c], rp[c]], axis=0), jnp.concatenate([bq[c], kq[c]], axis=0))
             for c in rng]
        a_ab = [jnp.where(strict, s[c][:c2, :c2], 0.0) for c in rng]
        a_ak = [jnp.where(strict, s[c][:c2, c2:], 0.0) for c in rng]
        a_rb = [jnp.where(incl, s[c][c2:, :c2], 0.0) for c in rng]
        a_rk = [jnp.where(incl, s[c][c2:, c2:], 0.0) for c in rng]
        yield
        akv = [_dot(a_ak[c], vs[c]) for c in rng]

        pw = [_dot(a_ab[c], a_ab[c]) for c in rng]
        tinv = [eye + a_ab[c] for c in rng]
        yield
        for _ in range(CHUNK_LOG2 - 2):
            pt = [_dot(pw[c], jnp.concatenate([pw[c], tinv[c]], axis=1)) for c in rng]
            pw = [pt[c][:, :c2] for c in rng]
            tinv = [tinv[c] + pt[c][:, c2:] for c in rng]
            yield
        q5 = [_dot(tinv[c], jnp.concatenate([ap[c], akv[c]], axis=1)) for c in rng]
        yield
        wu = [q5[c] + _dot(pw[c], q5[c]) for c in rng]
        yield
        x = [jnp.concatenate([wu[c], jnp.concatenate([zeros_cc, vs[c]], axis=1)], axis=0) for c in rng]
        ry = [_dot(jnp.concatenate([a_rb[c], a_rk[c]], axis=1), x[c]) for c in rng]
        mz = [_dot_tn(jnp.concatenate([bc[c], kc[c]], axis=0), x[c]) for c in rng]
        yield
        for c in rng:
            lhs_ref[wr, c] = jnp.concatenate([rp[c] + ry[c][:, :PAIR], mz[c][:, :PAIR]], axis=0).astype(_BF16)
            add_ref[wr, c] = jnp.concatenate([ry[c][:, PAIR:], mz[c][:, PAIR:]], axis=0)
        yield

    @pl.when(t < nt)
    def _():
        _interleave(prepare(), advance())

    @pl.when(t == nt)
    def _():
        _interleave(advance())


def _rwkv(pmat, lora_in, prm, batch, seq):
    tt = 512
    nt = seq // tt
    nc = tt // CHUNK
    cur = lambda b, t: b * nt + jnp.minimum(t, nt - 1)
    blk = lambda c0: pl.BlockSpec((tt, PAIR), lambda b, p, t, c0=c0: (cur(b, t), c0 + p))
    vec = pl.BlockSpec((1, PAIR), lambda b, p, t: (0, p))
    lora = lambda rows: pl.BlockSpec((rows, PAIR), lambda b, p, t: (0, p))
    c_r = 3 * D_HEADS // PAIR
    return pl.pallas_call(
        functools.partial(_rwkv_kernel, tt=tt, nt=nt),
        grid=(batch, N_PAIRS, nt + 1),
        in_specs=[blk(c_r), blk(c_r + N_PAIRS), blk(c_r + 2 * N_PAIRS),
                  pl.BlockSpec((tt, SMALL_W), lambda b, p, t: (cur(b, t), 0)),
                  vec, vec, vec, vec, vec, vec, vec,
                  lora(PAIR), lora(PAIR), lora(GATE_LORA)],
        out_specs=pl.BlockSpec((1, tt, PAIR), lambda b, p, t: (b, jnp.maximum(t - 1, 0), p)),
        out_shape=jax.ShapeDtypeStruct((batch, seq, D_HEADS), _BF16),
        scratch_shapes=[pltpu.VMEM((PAIR, PAIR), _F32),
                        pltpu.VMEM((2, nc, 2 * PAIR, PAIR), _BF16),
                        pltpu.VMEM((2, nc, 2 * PAIR, PAIR), _F32),
                        pltpu.VMEM((2, PAIR, PAIR), _F32),
                        pltpu.VMEM((2, tt, PAIR), _F32),
                        pltpu.VMEM((2, tt, PAIR), _F32)],
        compiler_params=_cparams(("parallel", "parallel", "arbitrary"), 48),
        name="rwkv",
    )(pmat, pmat, pmat, lora_in, prm["w0"], prm["a0"], prm["k_k"], prm["k_a"], prm["r_k"],
      prm["lnx_g"], prm["lnx_b"], prm["decay_b"], prm["a_b"], prm["g_b"])


def _outproj_kernel(fox_ref, rw_ref, x_ref, w1_ref, w2_ref, gt_ref, g2_ref, sh_ref, sc_ref,
                    x1_ref, h2_ref):
    y = (jnp.dot(fox_ref[...], w1_ref[...], preferred_element_type=_F32)
         + jnp.dot(rw_ref[...], w2_ref[...], preferred_element_type=_F32))
    x1 = x_ref[...] + gt_ref[0] * y
    x1_ref[...] = x1
    h = x1 * lax.rsqrt(jnp.mean(x1 * x1, axis=-1, keepdims=True) + NORM_EPS) * g2_ref[...]
    h2_ref[...] = (h * (1.0 + sc_ref[0]) + sh_ref[0]).astype(_BF16)


def _outproj(fox, rw, x2, w1, w2, gt, g2, sh, sc, seq):
    m, d = x2.shape
    tm = 512
    per_b = seq // tm
    rows = lambda w: pl.BlockSpec((tm, w), lambda i: (i, 0))
    full = lambda a: pl.BlockSpec(a.shape, lambda i: (0, 0))
    mod = pl.BlockSpec((1, 1, d), lambda i: (i // per_b, 0, 0))
    return pl.pallas_call(
        _outproj_kernel,
        grid=(m // tm,),
        in_specs=[rows(D_HEADS), rows(D_HEADS), rows(d), full(w1), full(w2), mod, full(g2), mod, mod],
        out_specs=[rows(d), rows(d)],
        out_shape=[jax.ShapeDtypeStruct((m, d), _F32), jax.ShapeDtypeStruct((m, d), _BF16)],
        compiler_params=_cparams(("parallel",), 48),
        name="outproj",
    )(fox, rw, x2, w1, w2, gt, g2, sh, sc)


def _mlp_kernel(h_ref, wu_ref, wd_ref, x1_ref, gt_ref, o_ref):
    @pl.when(pl.program_id(1) == 0)
    def _():
        o_ref[...] = x1_ref[...]

    u = jnp.dot(h_ref[...], wu_ref[...], preferred_element_type=_F32)
    act = jnp.square(jnp.maximum(u, 0.0)).astype(_BF16)
    o_ref[...] += gt_ref[0] * jnp.dot(act, wd_ref[...], preferred_element_type=_F32)


def _mlp(h2, wu, wd, x1, gt, seq):
    m, d = x1.shape
    ff = wu.shape[1]
    tm, tf = 1024, 1024
    per_b = seq // tm
    rows = lambda **kw: pl.BlockSpec((tm, d), lambda i, f: (i, 0), **kw)
    return pl.pallas_call(
        _mlp_kernel,
        grid=(m // tm, ff // tf),
        in_specs=[rows(),
                  pl.BlockSpec((d, tf), lambda i, f: (0, f)),
                  pl.BlockSpec((tf, d), lambda i, f: (f, 0)),
                  rows(),
                  pl.BlockSpec((1, 1, d), lambda i, f: (i // per_b, 0, 0))],
        out_specs=rows(pipeline_mode=pl.Buffered(1)),
        out_shape=jax.ShapeDtypeStruct((m, d), _F32),
        compiler_params=_cparams(("parallel", "arbitrary"), 58),
        name="mlp",
    )(h2, wu, wd, x1, gt)


def _pad_rows(a, rows):
    return jnp.pad(a, ((0, rows - a.shape[0]), (0, 0)))


def _pad_lanes(a, width):
    return jnp.pad(a, ((0, 0), (0, width - a.shape[1])))


def _pack_kernel(wt_ref, o_ref):
    o_f = 3 * D_HEADS
    o_r = o_f + N_HEADS
    o_w = o_r + 3 * D_HEADS
    o_a = o_w + DECAY_LORA
    o_g = o_a + ICLR_LORA
    lanes = wt_ref.shape[1]
    dst = 0
    for src, size in ((0, o_f), (o_r, o_w - o_r), (o_w, DECAY_LORA), (o_f, N_HEADS),
                      (None, PAIR - DECAY_LORA - N_HEADS), (o_a, ICLR_LORA), (None, PAIR - ICLR_LORA),
                      (o_g, GATE_LORA)):
        if src is None:
            o_ref[dst:dst + size, :] = jnp.zeros((size, lanes), _BF16)
        else:
            o_ref[dst:dst + size, :] = wt_ref[src:src + size, :].astype(_BF16)
        dst += size


def _pack_w_in(w):
    wt = w.T
    n, d = wt.shape
    tl = 256
    return pl.pallas_call(
        _pack_kernel,
        grid=(d // tl,),
        in_specs=[pl.BlockSpec((n, tl), lambda i: (0, i))],
        out_specs=pl.BlockSpec((6 * D_HEADS + SMALL_W, tl), lambda i: (0, i)),
        out_shape=jax.ShapeDtypeStruct((6 * D_HEADS + SMALL_W, d), _BF16),
        compiler_params=_cparams(("parallel",), 40),
        name="pack_w_in",
    )(wt)


def _pack_shift_mu(mu):
    o_w = 3 * D_HEADS
    o_a = o_w + DECAY_LORA
    o_g = o_a + ICLR_LORA
    return jnp.concatenate([jnp.zeros((1, 3 * D_HEADS), _F32), mu[:, :o_w], _pad_lanes(mu[:, o_w:o_a], PAIR),
                            _pad_lanes(mu[:, o_a:o_g], PAIR), mu[:, o_g:]], axis=1)


def kernel(x, c, w_ada, b_ada, norm1_g, w_in, b_forget, q_norm_g, k_norm_g, fox_out_g, shift_mu, w0, decay_b, a0, a_b, g_b, k_k, k_a, r_k, lnx_g, lnx_b, w_out, norm2_g, w_mlp_up, w_mlp_down):
    batch, seq, d = x.shape
    depth = w_ada.shape[0]
    x2 = x.reshape(batch * seq, d)
    row = lambda a: a.reshape(1, -1)
    for l in range(depth):
        mod = _ada(c, w_ada[l], b_ada[l])
        sh1, sc1, gt1, sh2, sc2, gt2 = [m.reshape(batch, 1, d) for m in jnp.split(mod, 6, axis=-1)]

        pmat = _proj(x2, row(norm1_g[l]), sh1, sc1, _pack_w_in(w_in[l]), _pack_shift_mu(row(shift_mu[l])), seq)

        bf_slot = jnp.pad(row(b_forget[l]), ((0, 0), (F_LANE0, PAIR - F_LANE0 - N_HEADS)))
        qt, ka, vt, lora_in = _fox_prep(pmat, jnp.tile(q_norm_g[l], 2).reshape(PAIR, 1),
                                        jnp.tile(k_norm_g[l], 2).reshape(PAIR, 1), bf_slot, batch, seq)
        fox = _fox_attn(qt, ka, vt, row(fox_out_g[l]), batch, seq)

        prm = {
            "w0": row(w0[l]), "a0": row(a0[l]), "k_k": row(k_k[l]), "k_a": row(k_a[l]),
            "r_k": row(r_k[l]), "lnx_g": row(lnx_g[l]), "lnx_b": row(lnx_b[l]),
            "decay_b": _pad_rows(decay_b[l], PAIR).astype(_BF16),
            "a_b": _pad_rows(a_b[l], PAIR).astype(_BF16),
            "g_b": g_b[l].astype(_BF16),
        }
        rw = _rwkv(pmat, lora_in, prm, batch, seq)

        w_o = w_out[l].astype(_BF16)
        x1, h2 = _outproj(fox.reshape(batch * seq, D_HEADS), rw.reshape(batch * seq, D_HEADS), x2,
                          w_o[:D_HEADS], w_o[D_HEADS:], gt1, row(norm2_g[l]), sh2, sc2, seq)
        x2 = _mlp(h2, w_mlp_up[l].astype(_BF16), w_mlp_down[l].astype(_BF16), x1, gt2, seq)
    return x2.reshape(batch, seq, d)
```

```python
import functools

import jax
import jax.numpy as jnp
from jax import lax
from jax.experimental import pallas as pl
from jax.experimental.pallas import tpu as pltpu

_F32 = jnp.float32
_BF16 = jnp.bfloat16

HEAD_DIM = 64
PAIR = 2 * HEAD_DIM
N_HEADS = 16
D_HEADS = N_HEADS * HEAD_DIM
N_PAIRS = N_HEADS // 2
DECAY_LORA = 96
ICLR_LORA = 96
GATE_LORA = 256
SMALL_W = 512
F_LANE0 = DECAY_LORA
NORM_EPS = 1e-6
LNX_EPS = 64e-5
NEG_INF = -1e30
LOG2E = 1.4426950408889634
CHUNK_LOG2 = 6
PAIRS_PER_STEP = 2
CHUNK = 1 << CHUNK_LOG2
MIB = 1024 * 1024


def _cparams(sem, vmem_mib):
    return pltpu.CompilerParams(dimension_semantics=sem, vmem_limit_bytes=vmem_mib * MIB)


def _dot(a, b):
    return jnp.dot(a.astype(_BF16), b.astype(_BF16), preferred_element_type=_F32)


def _dot_nt(a, b):
    return lax.dot_general(a.astype(_BF16), b.astype(_BF16), (((1,), (1,)), ((), ())),
                           preferred_element_type=_F32)


def _dot_tn(a, b):
    return lax.dot_general(a.astype(_BF16), b.astype(_BF16), (((0,), (0,)), ((), ())),
                           preferred_element_type=_F32)


def _split3(x):
    h = x.astype(_BF16).astype(_F32)
    r = x - h
    m = r.astype(_BF16).astype(_F32)
    l = (r - m).astype(_BF16).astype(_F32)
    return h, m, l


def _dot_exact_rhs(a_bf16, x):
    return sum(jnp.dot(a_bf16, p.astype(_BF16), preferred_element_type=_F32) for p in _split3(x))


def _sigmoid(x):
    return 1.0 / (1.0 + jnp.exp(-x))


def _softplus(x):
    return jnp.maximum(x, 0.0) + jnp.log(1.0 + jnp.exp(-jnp.abs(x)))


def _half_sum(x, lo_half):
    s_lo = jnp.sum(jnp.where(lo_half, x, 0.0), axis=-1, keepdims=True)
    s_hi = jnp.sum(jnp.where(lo_half, 0.0, x), axis=-1, keepdims=True)
    return jnp.where(lo_half, s_lo, s_hi)


def _lo_half_mask():
    return lax.broadcasted_iota(jnp.int32, (1, PAIR), 1) < HEAD_DIM


def _ada_kernel(c_ref, w_ref, b_ref, o_ref):
    c = c_ref[...]
    cond = c * _sigmoid(c)
    o_ref[...] = _dot(cond, w_ref[...]) + b_ref[...]


def _ada(c, w_ada, b_ada):
    b, d = c.shape
    n = w_ada.shape[1]
    rows = 8 * pl.cdiv(b, 8)
    tn = 1024
    cp = jnp.pad(c, ((0, rows - b), (0, 0)))
    out = pl.pallas_call(
        _ada_kernel,
        grid=(n // tn,),
        in_specs=[pl.BlockSpec((rows, d), lambda j: (0, 0)),
                  pl.BlockSpec((d, tn), lambda j: (0, j)),
                  pl.BlockSpec((1, tn), lambda j: (0, j))],
        out_specs=pl.BlockSpec((rows, tn), lambda j: (0, j)),
        out_shape=jax.ShapeDtypeStruct((rows, n), _F32),
        compiler_params=_cparams(("parallel",), 40),
        name="ada",
    )(cp, w_ada, b_ada.reshape(1, n))
    return out[:b]


NORM_ROWS = 128


def _adaln(x, g_ref, sh_ref, sc_ref):
    y = x * lax.rsqrt(jnp.mean(x * x, axis=-1, keepdims=True) + NORM_EPS)
    return (y * (g_ref[...] * (1.0 + sc_ref[0])) + sh_ref[0]).astype(_BF16)


def _proj_kernel(x_ref, g_ref, sh_ref, sc_ref, w_ref, mu_ref, o_ref, h_ref, carry_ref, *, tm, per_b):
    i = pl.program_id(0)
    j = pl.program_id(1)

    @pl.when(j == 0)
    def _():
        for c in range(tm // NORM_ROWS):
            rows = slice(c * NORM_ROWS, (c + 1) * NORM_ROWS)
            h_ref[rows, :] = _adaln(x_ref[rows, :], g_ref, sh_ref, sc_ref)

    y = lax.dot_general(h_ref[...], w_ref[...], (((1,), (1,)), ((), ())), preferred_element_type=_F32)
    first_row = lax.broadcasted_iota(jnp.int32, (tm, 1), 0) == 0
    above = jnp.where(lax.rem(i, per_b) == 0, 0.0, carry_ref[j, 0:1, :])
    prev = jnp.where(first_row, above, pltpu.roll(y, 1, axis=0))
    carry_ref[j, 0:1, :] = y[tm - 1:tm, :]
    o_ref[...] = y + (prev - y) * mu_ref[...]


def _proj(x2, g, sh, sc, wt, mu, seq):
    m, d = x2.shape
    n = wt.shape[0]
    tm, tn = 1024, 512
    per_b = seq // tm
    return pl.pallas_call(
        functools.partial(_proj_kernel, tm=tm, per_b=per_b),
        grid=(m // tm, n // tn),
        in_specs=[pl.BlockSpec((tm, d), lambda i, j: (i, 0)),
                  pl.BlockSpec((1, d), lambda i, j: (0, 0)),
                  pl.BlockSpec((1, 1, d), lambda i, j: (i // per_b, 0, 0)),
                  pl.BlockSpec((1, 1, d), lambda i, j: (i // per_b, 0, 0)),
                  pl.BlockSpec((tn, d), lambda i, j: (j, 0)),
                  pl.BlockSpec((1, tn), lambda i, j: (0, j))],
        out_specs=pl.BlockSpec((tm, tn), lambda i, j: (i, j)),
        out_shape=jax.ShapeDtypeStruct((m, n), _F32),
        scratch_shapes=[pltpu.VMEM((tm, d), _BF16), pltpu.VMEM((n // tn, 8, tn), _F32)],
        compiler_params=_cparams(("arbitrary", "arbitrary"), 48),
        name="proj",
    )(x2, g, sh, sc, wt, mu)


def _fox_prep_kernel(q_ref, k_ref, v_ref, s_ref, qg_ref, kg_ref, bf_ref, qt_ref, ka_ref, vt_ref,
                     lo_ref, carry_ref, *, tt):
    @pl.when(pl.program_id(1) == 0)
    def _():
        carry_ref[...] = jnp.zeros_like(carry_ref)

    lo_ref[:, 0:PAIR] = jnp.tanh(s_ref[:, 0:PAIR]).astype(_BF16)
    lo_ref[:, PAIR:2 * PAIR] = s_ref[:, PAIR:2 * PAIR].astype(_BF16)
    lo_ref[:, 2 * PAIR:] = _sigmoid(s_ref[:, 2 * PAIR:]).astype(_BF16)

    z = s_ref[:, 0:PAIR] + bf_ref[...]
    logf = -_softplus(-z)
    row = lax.broadcasted_iota(jnp.int32, (tt, tt), 0)
    col = lax.broadcasted_iota(jnp.int32, (tt, tt), 1)
    tri = jnp.where(row >= col, 1.0, 0.0).astype(_BF16)
    dcum = carry_ref[0:1, :] + _dot_exact_rhs(tri, logf)
    carry_ref[0:1, :] = dcum[tt - 1:tt, :]
    d1, d2, d3 = _split3((dcum * LOG2E).T[F_LANE0:F_LANE0 + N_HEADS, :])

    r8 = lax.broadcasted_iota(jnp.int32, (8, tt), 0)
    lo_rows = lax.broadcasted_iota(jnp.int32, (PAIR, 1), 0) < HEAD_DIM
    ones_row = jnp.where(r8 == 0, 1.0, 0.0)
    fill = jnp.zeros((HEAD_DIM - 8, tt), _F32)
    gq = jnp.broadcast_to(qg_ref[...] * (HEAD_DIM ** -0.5 * LOG2E), (PAIR, tt))
    gk = jnp.broadcast_to(kg_ref[...], (PAIR, tt))

    def normed(x_t, gain):
        sq = x_t * x_t
        ms = jnp.where(lo_rows, jnp.sum(sq[:HEAD_DIM], axis=0, keepdims=True),
                       jnp.sum(sq[HEAD_DIM:], axis=0, keepdims=True)) * (1.0 / HEAD_DIM)
        return x_t * lax.rsqrt(ms + NORM_EPS) * gain

    for p in range(N_PAIRS):
        cols = slice(p * PAIR, (p + 1) * PAIR)
        qn = normed(q_ref[:, cols].T, gq)
        kn = normed(k_ref[:, cols].T, gk)
        vt = v_ref[:, cols].T
        for hh in range(2):
            h = 2 * p + hh
            a1, a2, a3 = d1[h:h + 1], d2[h:h + 1], d3[h:h + 1]
            aq = jnp.where(r8 < 3, 1.0, jnp.where(r8 == 3, a1, jnp.where(r8 == 4, a2, jnp.where(r8 == 5, a3, 0.0))))
            ak = jnp.where(r8 == 0, -a1, jnp.where(r8 == 1, -a2, jnp.where(r8 == 2, -a3, jnp.where(r8 < 6, 1.0, 0.0))))

            def with_rows(x_t, extra, hh=hh):
                if hh == 0:
                    return jnp.concatenate([x_t[:HEAD_DIM], extra, fill], axis=0)
                return jnp.concatenate([extra, fill, x_t[HEAD_DIM:]], axis=0)

            qt_ref[0, h] = with_rows(qn, aq).astype(_BF16)
            ka_ref[0, h] = with_rows(kn, ak).T.astype(_BF16)
            vt_ref[0, h] = with_rows(vt, ones_row).astype(_BF16)


def _fox_prep(pmat, qg2, kg2, bf_slot, batch, seq):
    tt = 256
    nt = seq // tt
    row_map = lambda c: (lambda b, t: (b * nt + t, c))
    row_sd = jax.ShapeDtypeStruct((batch, N_HEADS, seq, PAIR), _BF16)
    col_sd = jax.ShapeDtypeStruct((batch, N_HEADS, PAIR, seq), _BF16)
    row_spec = pl.BlockSpec((1, N_HEADS, tt, PAIR), lambda b, t: (b, 0, t, 0))
    col_spec = pl.BlockSpec((1, N_HEADS, PAIR, tt), lambda b, t: (b, 0, 0, t))
    vec = pl.BlockSpec((1, PAIR), lambda b, t: (0, 0))
    colvec = pl.BlockSpec((PAIR, 1), lambda b, t: (0, 0))
    return pl.pallas_call(
        functools.partial(_fox_prep_kernel, tt=tt),
        grid=(batch, nt),
        in_specs=[pl.BlockSpec((tt, D_HEADS), row_map(0)),
                  pl.BlockSpec((tt, D_HEADS), row_map(1)),
                  pl.BlockSpec((tt, D_HEADS), row_map(2)),
                  pl.BlockSpec((tt, SMALL_W), row_map(6 * D_HEADS // SMALL_W)),
                  colvec, colvec, vec],
        out_specs=[col_spec, row_spec, col_spec, pl.BlockSpec((tt, SMALL_W), row_map(0))],
        out_shape=[col_sd, row_sd, col_sd, jax.ShapeDtypeStruct((batch * seq, SMALL_W), _BF16)],
        scratch_shapes=[pltpu.VMEM((8, PAIR), _F32)],
        compiler_params=_cparams(("parallel", "arbitrary"), 40),
        name="fox_prep",
    )(pmat, pmat, pmat, pmat, qg2, kg2, bf_slot)


def _fox_attn_kernel(qt_ref, k_ref, vt_ref, g_ref, o_ref, m_ref, acc_ref, s_ref, *, tq, tk):
    qi = pl.program_id(2)
    key = lax.broadcasted_iota(jnp.int32, (tk, tq), 0)
    qry = lax.broadcasted_iota(jnp.int32, (tk, tq), 1)
    m_ref[...] = jnp.full(m_ref.shape, NEG_INF, _F32)
    acc_ref[...] = jnp.zeros(acc_ref.shape, _F32)

    def scores(blk, slot, hh, lanes=slice(None)):
        k = k_ref[0, hh, pl.ds(pl.multiple_of(blk * tk, tk), tk), :]
        s_ref[slot, hh, :, lanes] = jnp.dot(k, qt_ref[0, hh, :, lanes],
                                            preferred_element_type=_F32)

    def absorb(blk, slot, hh, mask=None, lanes=slice(None)):
        vt = vt_ref[0, hh, :, pl.ds(pl.multiple_of(blk * tk, tk), tk)]
        st = s_ref[slot, hh, :, lanes]
        if mask is not None:
            st = jnp.where(mask, st, NEG_INF)
        m_old = m_ref[hh, :, lanes]
        m_new = jnp.maximum(m_old, jnp.max(st, axis=0, keepdims=True))
        p = jnp.exp2(st - m_new)
        acc_ref[hh, :, lanes] = jnp.exp2(m_old - m_new) * acc_ref[hh, :, lanes] + jnp.dot(
            vt, p.astype(_BF16), preferred_element_type=_F32)
        m_ref[hh, :, lanes] = m_new

    for hh in range(2):
        scores(0, 0, hh)

    def two_blocks(j2, carry):
        blk = 2 * j2
        for hh in range(2):
            scores(blk + 1, 1, hh)
            absorb(blk, 0, hh)
        for hh in range(2):
            scores(blk + 2, 0, hh)
            absorb(blk + 1, 1, hh)
        return carry

    lax.fori_loop(0, qi, two_blocks, 0)
    upper = slice(tk, tq)
    for hh in range(2):
        scores(2 * qi + 1, 1, hh, upper)
        absorb(2 * qi, 0, hh, key <= qry)
    for hh in range(2):
        absorb(2 * qi + 1, 1, hh, (key <= qry)[:, :tk], upper)

    acc0, acc1 = acc_ref[0], acc_ref[1]
    o0 = acc0[:HEAD_DIM] / acc0[HEAD_DIM:HEAD_DIM + 1]
    o1 = acc1[HEAD_DIM:] / acc1[0:1]
    n0 = o0 * lax.rsqrt(jnp.mean(o0 * o0, axis=0, keepdims=True) + NORM_EPS)
    n1 = o1 * lax.rsqrt(jnp.mean(o1 * o1, axis=0, keepdims=True) + NORM_EPS)
    o_ref[0] = (jnp.concatenate([n0, n1], axis=0).T * g_ref[...]).astype(_BF16)


def _fox_attn(qt, ka, vt, og, batch, seq):
    tk = 512
    tq = 2 * tk
    nq = seq // tq
    return pl.pallas_call(
        functools.partial(_fox_attn_kernel, tq=tq, tk=tk),
        grid=(batch, N_PAIRS, nq),
        in_specs=[pl.BlockSpec((1, 2, PAIR, tq), lambda b, p, i: (b, p, 0, i)),
                  pl.BlockSpec((1, 2, seq, PAIR), lambda b, p, i: (b, p, 0, 0)),
                  pl.BlockSpec((1, 2, PAIR, seq), lambda b, p, i: (b, p, 0, 0)),
                  pl.BlockSpec((1, PAIR), lambda b, p, i: (0, p))],
        out_specs=pl.BlockSpec((1, tq, PAIR), lambda b, p, i: (b, i, p)),
        out_shape=jax.ShapeDtypeStruct((batch, seq, D_HEADS), _BF16),
        scratch_shapes=[pltpu.VMEM((2, 1, tq), _F32), pltpu.VMEM((2, PAIR, tq), _F32),
                        pltpu.VMEM((2, 2, tk, tq), _F32)],
        compiler_params=_cparams(("parallel", "parallel", "arbitrary"), 48),
        name="fox_attn",
    )(qt, ka, vt, og)


def _stack_heads(x, lo_half):
    return jnp.concatenate([jnp.where(lo_half, x, 0.0), jnp.where(lo_half, 0.0, x)], axis=0)


def _interleave(*gens):
    live = list(gens)
    while live:
        for gen in list(live):
            try:
                next(gen)
            except StopIteration:
                live.remove(gen)


def _rwkv_kernel(r_ref, k_ref, v_ref, lo_ref, w0_ref, a0_ref, kk_ref, ka_ref, rk_ref, lg_ref, lb_ref,
                 db_ref, ab_ref, gb_ref, o_ref, z_ref, lhs_ref, add_ref, dec_ref, bv_ref, gate_ref,
                 *, tt, nt):
    nc = tt // CHUNK
    c2 = 2 * CHUNK
    t = pl.program_id(2)
    wr = lax.rem(t, 2)
    rd = 1 - wr
    lo_half = _lo_half_mask()

    @pl.when(t == 0)
    def _():
        z_ref[...] = jnp.zeros_like(z_ref)
        lhs_ref[:, 1] = jnp.zeros((PAIRS_PER_STEP,) + lhs_ref.shape[2:], lhs_ref.dtype)
        add_ref[:, 1] = jnp.zeros((PAIRS_PER_STEP,) + add_ref.shape[2:], add_ref.dtype)
        dec_ref[:, 1] = jnp.ones((PAIRS_PER_STEP,) + dec_ref.shape[2:], dec_ref.dtype)
        bv_ref[:, 1] = jnp.zeros((PAIRS_PER_STEP,) + bv_ref.shape[2:], bv_ref.dtype)
        gate_ref[:, 1] = jnp.zeros((PAIRS_PER_STEP,) + gate_ref.shape[2:], gate_ref.dtype)

    def advance(q):
        lanes = slice(q * PAIR, (q + 1) * PAIR)
        z = z_ref[q]
        dec = dec_ref[q, rd]
        ys = []
        for c in range(nc):
            res = jnp.dot(lhs_ref[q, rd, c], z.astype(_BF16), preferred_element_type=_F32)
            add = add_ref[q, rd, c]
            ysh = res[:c2] + add[:c2]
            ys.append(ysh[:CHUNK] + ysh[CHUNK:])
            z = dec[:, c:c + 1] * z + res[c2:] + add[c2:]
            yield
        z_ref[q] = z
        y = jnp.concatenate(ys, axis=0)
        mu = _half_sum(y, lo_half) * (1.0 / HEAD_DIM)
        dlt = y - mu
        var = _half_sum(dlt * dlt, lo_half) * (1.0 / HEAD_DIM)
        yn = dlt * lax.rsqrt(var + LNX_EPS) * lg_ref[:, lanes] + lb_ref[:, lanes]
        o_ref[0, :, lanes] = ((yn + bv_ref[q, rd]) * gate_ref[q, rd]).astype(_BF16)
        yield

    def prepare(q):
        lanes = slice(q * PAIR, (q + 1) * PAIR)
        r = r_ref[:, lanes]
        k = k_ref[:, lanes]
        v = v_ref[:, lanes]
        w = -_softplus(-(w0_ref[:, lanes] + jnp.dot(lo_ref[:, 0:PAIR], db_ref[:, lanes],
                                                      preferred_element_type=_F32))) - 0.5
        logd = -jnp.exp(w)
        a = _sigmoid(a0_ref[:, lanes] + jnp.dot(lo_ref[:, PAIR:2 * PAIR], ab_ref[:, lanes],
                                                preferred_element_type=_F32))
        gate_ref[q, wr] = jnp.dot(lo_ref[:, 2 * PAIR:], gb_ref[:, lanes], preferred_element_type=_F32)
        kk = k * kk_ref[:, lanes]
        kk = kk * lax.rsqrt(jnp.maximum(_half_sum(kk * kk, lo_half), 1e-24))
        k = k * (1.0 + (a - 1.0) * ka_ref[:, lanes])
        bb = kk * a
        bv_ref[q, wr] = _half_sum(r * k * rk_ref[:, lanes], lo_half) * v
        yield

        pos = lax.broadcasted_iota(jnp.int32, (tt, 1), 0) & (CHUNK - 1)
        lcum = logd
        for j in range(CHUNK_LOG2):
            lcum = lcum + jnp.where(pos >= (1 << j), pltpu.roll(lcum, 1 << j, axis=0), 0.0)
        yield

        r2 = lax.broadcasted_iota(jnp.int32, (c2, c2), 0)
        q2 = lax.broadcasted_iota(jnp.int32, (c2, c2), 1)
        same = (r2 >> CHUNK_LOG2) == (q2 >> CHUNK_LOG2)
        strict = same & (r2 > q2)
        incl = same & (r2 >= q2)
        eye = jnp.where(r2 == q2, 1.0, 0.0)
        zeros_cc = jnp.zeros((c2, PAIR), _F32)

        rng = range(nc)
        sl = [slice(c * CHUNK, (c + 1) * CHUNK) for c in rng]
        lc = [lcum[sl[c]] for c in rng]
        lc_rows = [lc[c][CHUNK - 1:CHUNK, :] for c in rng]
        dec_ref[q, wr] = jnp.exp(jnp.concatenate(lc_rows + [jnp.zeros((PAIR - nc, PAIR), _F32)], axis=0).T)
        e_pos = [jnp.exp(lc[c]) for c in rng]
        e_neg = [jnp.exp(-lc[c]) for c in rng]
        e_end = [jnp.exp(lc_rows[c] - lc[c]) for c in rng]
        rp = [_stack_heads(r[sl[c]] * e_pos[c], lo_half) for c in rng]
        ap = [_stack_heads(-kk[sl[c]] * jnp.exp(lc[c] - logd[sl[c]]), lo_half) for c in rng]
        kq = [_stack_heads(k[sl[c]] * e_neg[c], lo_half) for c in rng]
        bq = [_stack_heads(bb[sl[c]] * e_neg[c], lo_half) for c in rng]
        kc = [_stack_heads(k[sl[c]] * e_end[c], lo_half) for c in rng]
        bc = [_stack_heads(bb[sl[c]] * e_end[c], lo_half) for c in rng]
        vs = [_stack_heads(v[sl[c]], lo_half) for c in rng]
        yield

        s = [_dot_nt(jnp.concatenate([ap[c], rp[c]], axis=0), jnp.concatenate([bq[c], kq[c]], axis=0))
             for c in rng]
        a_ab = [jnp.where(strict, s[c][:c2, :c2], 0.0) for c in rng]
        a_ak = [jnp.where(strict, s[c][:c2, c2:], 0.0) for c in rng]
        a_rb = [jnp.where(incl, s[c][c2:, :c2], 0.0) for c in rng]
        a_rk = [jnp.where(incl, s[c][c2:, c2:], 0.0) for c in rng]
        yield
        akv = [_dot(a_ak[c], vs[c]) for c in rng]

        pw = [_dot(a_ab[c], a_ab[c]) for c in rng]
        tinv = [eye + a_ab[c] for c in rng]
        yield
        for _ in range(CHUNK_LOG2 - 2):
            pt = [_dot(pw[c], jnp.concatenate([pw[c], tinv[c]], axis=1)) for c in rng]
            pw = [pt[c][:, :c2] for c in rng]
            tinv = [tinv[c] + pt[c][:, c2:] for c in rng]
            yield
        q5 = [_dot(tinv[c], jnp.concatenate([ap[c], akv[c]], axis=1)) for c in rng]
        yield
        wu = [q5[c] + _dot(pw[c], q5[c]) for c in rng]
        yield
        x = [jnp.concatenate([wu[c], jnp.concatenate([zeros_cc, vs[c]], axis=1)], axis=0) for c in rng]
        ry = [_dot(jnp.concatenate([a_rb[c], a_rk[c]], axis=1), x[c]) for c in rng]
        mz = [_dot_tn(jnp.concatenate([bc[c], kc[c]], axis=0), x[c]) for c in rng]
        yield
        for c in rng:
            lhs_ref[q, wr, c] = jnp.concatenate([rp[c] + ry[c][:, :PAIR], mz[c][:, :PAIR]], axis=0).astype(_BF16)
            add_ref[q, wr, c] = jnp.concatenate([ry[c][:, PAIR:], mz[c][:, PAIR:]], axis=0)
        yield

    pairs = range(PAIRS_PER_STEP)

    @pl.when(t < nt)
    def _():
        _interleave(*[prepare(q) for q in pairs], *[advance(q) for q in pairs])

    @pl.when(t == nt)
    def _():
        _interleave(*[advance(q) for q in pairs])


def _rwkv(pmat, lora_in, prm, batch, seq):
    tt = 512
    nt = seq // tt
    nc = tt // CHUNK
    npp = PAIRS_PER_STEP
    width = npp * PAIR
    cur = lambda b, t: b * nt + jnp.minimum(t, nt - 1)
    blk = lambda c0: pl.BlockSpec((tt, width), lambda b, p, t, c0=c0: (cur(b, t), c0 + p))
    vec = pl.BlockSpec((1, width), lambda b, p, t: (0, p))
    lora = lambda rows: pl.BlockSpec((rows, width), lambda b, p, t: (0, p))
    c_r = 3 * D_HEADS // width
    per = D_HEADS // width
    return pl.pallas_call(
        functools.partial(_rwkv_kernel, tt=tt, nt=nt),
        grid=(batch, N_PAIRS // npp, nt + 1),
        in_specs=[blk(c_r), blk(c_r + per), blk(c_r + 2 * per),
                  pl.BlockSpec((tt, SMALL_W), lambda b, p, t: (cur(b, t), 0)),
                  vec, vec, vec, vec, vec, vec, vec,
                  lora(PAIR), lora(PAIR), lora(GATE_LORA)],
        out_specs=pl.BlockSpec((1, tt, width), lambda b, p, t: (b, jnp.maximum(t - 1, 0), p)),
        out_shape=jax.ShapeDtypeStruct((batch, seq, D_HEADS), _BF16),
        scratch_shapes=[pltpu.VMEM((npp, PAIR, PAIR), _F32),
                        pltpu.VMEM((npp, 2, nc, 2 * PAIR, PAIR), _BF16),
                        pltpu.VMEM((npp, 2, nc, 2 * PAIR, PAIR), _F32),
                        pltpu.VMEM((npp, 2, PAIR, PAIR), _F32),
                        pltpu.VMEM((npp, 2, tt, PAIR), _F32),
                        pltpu.VMEM((npp, 2, tt, PAIR), _F32)],
        compiler_params=_cparams(("parallel", "parallel", "arbitrary"), 48),
        name="rwkv",
    )(pmat, pmat, pmat, lora_in, prm["w0"], prm["a0"], prm["k_k"], prm["k_a"], prm["r_k"],
      prm["lnx_g"], prm["lnx_b"], prm["decay_b"], prm["a_b"], prm["g_b"])


def _outproj_kernel(fox_ref, rw_ref, x_ref, w1_ref, w2_ref, gt_ref, g2_ref, sh_ref, sc_ref,
                    x1_ref, h2_ref):
    for c in range(x_ref.shape[0] // NORM_ROWS):
        rows = slice(c * NORM_ROWS, (c + 1) * NORM_ROWS)
        y = (jnp.dot(fox_ref[rows, :], w1_ref[...], preferred_element_type=_F32)
             + jnp.dot(rw_ref[rows, :], w2_ref[...], preferred_element_type=_F32))
        x1 = x_ref[rows, :] + gt_ref[0] * y
        x1_ref[rows, :] = x1
        h2_ref[rows, :] = _adaln(x1, g2_ref, sh_ref, sc_ref)


def _outproj(fox, rw, x2, w1, w2, gt, g2, sh, sc, seq):
    m, d = x2.shape
    tm = 512
    per_b = seq // tm
    rows = lambda w: pl.BlockSpec((tm, w), lambda i: (i, 0))
    full = lambda a: pl.BlockSpec(a.shape, lambda i: (0, 0))
    mod = pl.BlockSpec((1, 1, d), lambda i: (i // per_b, 0, 0))
    return pl.pallas_call(
        _outproj_kernel,
        grid=(m // tm,),
        in_specs=[rows(D_HEADS), rows(D_HEADS), rows(d), full(w1), full(w2), mod, full(g2), mod, mod],
        out_specs=[rows(d), rows(d)],
        out_shape=[jax.ShapeDtypeStruct((m, d), _F32), jax.ShapeDtypeStruct((m, d), _BF16)],
        compiler_params=_cparams(("parallel",), 48),
        name="outproj",
    )(fox, rw, x2, w1, w2, gt, g2, sh, sc)


def _mlp_kernel(h_ref, wu_ref, wd_ref, x1_ref, gt_ref, o_ref):
    @pl.when(pl.program_id(1) == 0)
    def _():
        o_ref[...] = x1_ref[...]

    u = jnp.dot(h_ref[...], wu_ref[...], preferred_element_type=_F32)
    act = jnp.square(jnp.maximum(u, 0.0)).astype(_BF16)
    o_ref[...] += gt_ref[0] * jnp.dot(act, wd_ref[...], preferred_element_type=_F32)


def _mlp(h2, wu, wd, x1, gt, seq):
    m, d = x1.shape
    ff = wu.shape[1]
    tm, tf = 1024, 1024
    per_b = seq // tm
    rows = lambda **kw: pl.BlockSpec((tm, d), lambda i, f: (i, 0), **kw)
    return pl.pallas_call(
        _mlp_kernel,
        grid=(m // tm, ff // tf),
        in_specs=[rows(),
                  pl.BlockSpec((d, tf), lambda i, f: (0, f)),
                  pl.BlockSpec((tf, d), lambda i, f: (f, 0)),
                  rows(),
                  pl.BlockSpec((1, 1, d), lambda i, f: (i // per_b, 0, 0))],
        out_specs=rows(pipeline_mode=pl.Buffered(1)),
        out_shape=jax.ShapeDtypeStruct((m, d), _F32),
        compiler_params=_cparams(("parallel", "arbitrary"), 58),
        name="mlp",
    )(h2, wu, wd, x1, gt)


def _pad_rows(a, rows):
    return jnp.pad(a, ((0, rows - a.shape[0]), (0, 0)))


def _pad_lanes(a, width):
    return jnp.pad(a, ((0, 0), (0, width - a.shape[1])))


def _pack_kernel(wt_ref, o_ref):
    o_f = 3 * D_HEADS
    o_r = o_f + N_HEADS
    o_w = o_r + 3 * D_HEADS
    o_a = o_w + DECAY_LORA
    o_g = o_a + ICLR_LORA
    lanes = wt_ref.shape[1]
    dst = 0
    for src, size in ((0, o_f), (o_r, o_w - o_r), (o_w, DECAY_LORA), (o_f, N_HEADS),
                      (None, PAIR - DECAY_LORA - N_HEADS), (o_a, ICLR_LORA), (None, PAIR - ICLR_LORA),
                      (o_g, GATE_LORA)):
        if src is None:
            o_ref[dst:dst + size, :] = jnp.zeros((size, lanes), _BF16)
        else:
            o_ref[dst:dst + size, :] = wt_ref[src:src + size, :].astype(_BF16)
        dst += size


def _pack_w_in(w):
    wt = w.T
    n, d = wt.shape
    tl = 256
    return pl.pallas_call(
        _pack_kernel,
        grid=(d // tl,),
        in_specs=[pl.BlockSpec((n, tl), lambda i: (0, i))],
        out_specs=pl.BlockSpec((6 * D_HEADS + SMALL_W, tl), lambda i: (0, i)),
        out_shape=jax.ShapeDtypeStruct((6 * D_HEADS + SMALL_W, d), _BF16),
        compiler_params=_cparams(("parallel",), 40),
        name="pack_w_in",
    )(wt)


def _pack_shift_mu(mu):
    o_w = 3 * D_HEADS
    o_a = o_w + DECAY_LORA
    o_g = o_a + ICLR_LORA
    return jnp.concatenate([jnp.zeros((1, 3 * D_HEADS), _F32), mu[:, :o_w], _pad_lanes(mu[:, o_w:o_a], PAIR),
                            _pad_lanes(mu[:, o_a:o_g], PAIR), mu[:, o_g:]], axis=1)


def kernel(x, c, w_ada, b_ada, norm1_g, w_in, b_forget, q_norm_g, k_norm_g, fox_out_g, shift_mu, w0, decay_b, a0, a_b, g_b, k_k, k_a, r_k, lnx_g, lnx_b, w_out, norm2_g, w_mlp_up, w_mlp_down):
    batch, seq, d = x.shape
    depth = w_ada.shape[0]
    x2 = x.reshape(batch * seq, d)
    row = lambda a: a.reshape(1, -1)
    for l in range(depth):
        mod = _ada(c, w_ada[l], b_ada[l])
        sh1, sc1, gt1, sh2, sc2, gt2 = [m.reshape(batch, 1, d) for m in jnp.split(mod, 6, axis=-1)]

        pmat = _proj(x2, row(norm1_g[l]), sh1, sc1, _pack_w_in(w_in[l]), _pack_shift_mu(row(shift_mu[l])), seq)

        bf_slot = jnp.pad(row(b_forget[l]), ((0, 0), (F_LANE0, PAIR - F_LANE0 - N_HEADS)))
        qt, ka, vt, lora_in = _fox_prep(pmat, jnp.tile(q_norm_g[l], 2).reshape(PAIR, 1),
                                        jnp.tile(k_norm_g[l], 2).reshape(PAIR, 1), bf_slot, batch, seq)
        fox = _fox_attn(qt, ka, vt, row(fox_out_g[l]), batch, seq)

        prm = {
            "w0": row(w0[l]), "a0": row(a0[l]), "k_k": row(k_k[l]), "k_a": row(k_a[l]),
            "r_k": row(r_k[l]), "lnx_g": row(lnx_g[l]), "lnx_b": row(lnx_b[l]),
            "decay_b": _pad_rows(decay_b[l], PAIR).astype(_BF16),
            "a_b": _pad_rows(a_b[l], PAIR).astype(_BF16),
            "g_b": g_b[l].astype(_BF16),
        }
        rw = _rwkv(pmat, lora_in, prm, batch, seq)

        w_o = w_out[l].astype(_BF16)
        x1, h2 = _outproj(fox.reshape(batch * seq, D_HEADS), rw.reshape(batch * seq, D_HEADS), x2,
                          w_o[:D_HEADS], w_o[D_HEADS:], gt1, row(norm2_g[l]), sh2, sc2, seq)
        x2 = _mlp(h2, w_mlp_up[l].astype(_BF16), w_mlp_down[l].astype(_BF16), x1, gt2, seq)
    return x2.reshape(batch, seq, d)
```

```python
import functools

import jax
import jax.numpy as jnp
from jax import lax
from jax.experimental import pallas as pl
from jax.experimental.pallas import tpu as pltpu

_F32 = jnp.float32
_BF16 = jnp.bfloat16

HEAD_DIM = 64
PAIR = 2 * HEAD_DIM
N_HEADS = 16
D_HEADS = N_HEADS * HEAD_DIM
N_PAIRS = N_HEADS // 2
DECAY_LORA = 96
ICLR_LORA = 96
GATE_LORA = 256
SMALL_W = 512
F_LANE0 = DECAY_LORA
NORM_EPS = 1e-6
LNX_EPS = 64e-5
NEG_INF = -1e30
LOG2E = 1.4426950408889634
CHUNK_LOG2 = 6
FOX_ROWS = HEAD_DIM + 16
PAIRS_PER_STEP = 4
CHUNK = 1 << CHUNK_LOG2
MIB = 1024 * 1024


def _cparams(sem, vmem_mib):
    return pltpu.CompilerParams(dimension_semantics=sem, vmem_limit_bytes=vmem_mib * MIB)


def _dot(a, b):
    return jnp.dot(a.astype(_BF16), b.astype(_BF16), preferred_element_type=_F32)


def _dot_nt(a, b):
    return lax.dot_general(a.astype(_BF16), b.astype(_BF16), (((1,), (1,)), ((), ())),
                           preferred_element_type=_F32)


def _dot_tn(a, b):
    return lax.dot_general(a.astype(_BF16), b.astype(_BF16), (((0,), (0,)), ((), ())),
                           preferred_element_type=_F32)


def _split3(x):
    h = x.astype(_BF16).astype(_F32)
    r = x - h
    m = r.astype(_BF16).astype(_F32)
    l = (r - m).astype(_BF16).astype(_F32)
    return h, m, l


def _dot_exact_rhs(a_bf16, x):
    return sum(jnp.dot(a_bf16, p.astype(_BF16), preferred_element_type=_F32) for p in _split3(x))


def _sigmoid(x):
    return 1.0 / (1.0 + jnp.exp(-x))


def _softplus(x):
    return jnp.maximum(x, 0.0) + jnp.log(1.0 + jnp.exp(-jnp.abs(x)))


def _half_sum(x, lo_half):
    s_lo = jnp.sum(jnp.where(lo_half, x, 0.0), axis=-1, keepdims=True)
    s_hi = jnp.sum(jnp.where(lo_half, 0.0, x), axis=-1, keepdims=True)
    return jnp.where(lo_half, s_lo, s_hi)


def _lo_half_mask():
    return lax.broadcasted_iota(jnp.int32, (1, PAIR), 1) < HEAD_DIM


def _ada_kernel(c_ref, w_ref, b_ref, o_ref):
    c = c_ref[...]
    cond = c * _sigmoid(c)
    o_ref[...] = _dot(cond, w_ref[...]) + b_ref[...]


def _ada(c, w_ada, b_ada):
    b, d = c.shape
    n = w_ada.shape[1]
    rows = 8 * pl.cdiv(b, 8)
    tn = 1024
    cp = jnp.pad(c, ((0, rows - b), (0, 0)))
    out = pl.pallas_call(
        _ada_kernel,
        grid=(n // tn,),
        in_specs=[pl.BlockSpec((rows, d), lambda j: (0, 0)),
                  pl.BlockSpec((d, tn), lambda j: (0, j)),
                  pl.BlockSpec((1, tn), lambda j: (0, j))],
        out_specs=pl.BlockSpec((rows, tn), lambda j: (0, j)),
        out_shape=jax.ShapeDtypeStruct((rows, n), _F32),
        compiler_params=_cparams(("parallel",), 40),
        name="ada",
    )(cp, w_ada, b_ada.reshape(1, n))
    return out[:b]


NORM_ROWS = 128


def _adaln(x, g_ref, sh_ref, sc_ref):
    y = x * lax.rsqrt(jnp.mean(x * x, axis=-1, keepdims=True) + NORM_EPS)
    return (y * (g_ref[...] * (1.0 + sc_ref[0])) + sh_ref[0]).astype(_BF16)


def _proj_kernel(x_ref, g_ref, sh_ref, sc_ref, w_ref, mu_ref, o_ref, h_ref, carry_ref, *, tm, per_b):
    i = pl.program_id(0)
    j = pl.program_id(1)

    @pl.when(j == 0)
    def _():
        for c in range(tm // NORM_ROWS):
            rows = slice(c * NORM_ROWS, (c + 1) * NORM_ROWS)
            h_ref[rows, :] = _adaln(x_ref[rows, :], g_ref, sh_ref, sc_ref)

    y = lax.dot_general(h_ref[...], w_ref[...], (((1,), (1,)), ((), ())), preferred_element_type=_F32)
    first_row = lax.broadcasted_iota(jnp.int32, (tm, 1), 0) == 0
    above = jnp.where(lax.rem(i, per_b) == 0, 0.0, carry_ref[j, 0:1, :])
    prev = jnp.where(first_row, above, pltpu.roll(y, 1, axis=0))
    carry_ref[j, 0:1, :] = y[tm - 1:tm, :]
    o_ref[...] = y + (prev - y) * mu_ref[...]


def _proj(x2, g, sh, sc, wt, mu, seq):
    m, d = x2.shape
    n = wt.shape[0]
    tm, tn = 1024, 512
    per_b = seq // tm
    return pl.pallas_call(
        functools.partial(_proj_kernel, tm=tm, per_b=per_b),
        grid=(m // tm, n // tn),
        in_specs=[pl.BlockSpec((tm, d), lambda i, j: (i, 0)),
                  pl.BlockSpec((1, d), lambda i, j: (0, 0)),
                  pl.BlockSpec((1, 1, d), lambda i, j: (i // per_b, 0, 0)),
                  pl.BlockSpec((1, 1, d), lambda i, j: (i // per_b, 0, 0)),
                  pl.BlockSpec((tn, d), lambda i, j: (j, 0)),
                  pl.BlockSpec((1, tn), lambda i, j: (0, j))],
        out_specs=pl.BlockSpec((tm, tn), lambda i, j: (i, j)),
        out_shape=jax.ShapeDtypeStruct((m, n), _F32),
        scratch_shapes=[pltpu.VMEM((tm, d), _BF16), pltpu.VMEM((n // tn, 8, tn), _F32)],
        compiler_params=_cparams(("arbitrary", "arbitrary"), 48),
        name="proj",
    )(x2, g, sh, sc, wt, mu)


def _fox_prep_kernel(q_ref, k_ref, v_ref, s_ref, qg_ref, kg_ref, bf_ref, qt_ref, ka_ref, vt_ref,
                     lo_ref, carry_ref, *, tt):
    @pl.when(pl.program_id(1) == 0)
    def _():
        carry_ref[...] = jnp.zeros_like(carry_ref)

    lo_ref[:, 0:PAIR] = jnp.tanh(s_ref[:, 0:PAIR]).astype(_BF16)
    lo_ref[:, PAIR:2 * PAIR] = s_ref[:, PAIR:2 * PAIR].astype(_BF16)
    lo_ref[:, 2 * PAIR:] = _sigmoid(s_ref[:, 2 * PAIR:]).astype(_BF16)

    z = s_ref[:, 0:PAIR] + bf_ref[...]
    logf = -_softplus(-z)
    row = lax.broadcasted_iota(jnp.int32, (tt, tt), 0)
    col = lax.broadcasted_iota(jnp.int32, (tt, tt), 1)
    tri = jnp.where(row >= col, 1.0, 0.0).astype(_BF16)
    dcum = carry_ref[0:1, :] + _dot_exact_rhs(tri, logf)
    carry_ref[0:1, :] = dcum[tt - 1:tt, :]
    d1, d2, d3 = _split3((dcum * LOG2E).T[F_LANE0:F_LANE0 + N_HEADS, :])

    r8 = lax.broadcasted_iota(jnp.int32, (8, tt), 0)
    lo_rows = lax.broadcasted_iota(jnp.int32, (PAIR, 1), 0) < HEAD_DIM
    ones_row = jnp.where(r8 == 0, 1.0, 0.0)
    fill = jnp.zeros((FOX_ROWS - HEAD_DIM - 8, tt), _F32)
    wide = jnp.zeros((PAIR - FOX_ROWS, tt), _F32)
    gq = jnp.broadcast_to(qg_ref[...] * (HEAD_DIM ** -0.5 * LOG2E), (PAIR, tt))
    gk = jnp.broadcast_to(kg_ref[...], (PAIR, tt))

    def normed(x_t, gain):
        sq = x_t * x_t
        ms = jnp.where(lo_rows, jnp.sum(sq[:HEAD_DIM], axis=0, keepdims=True),
                       jnp.sum(sq[HEAD_DIM:], axis=0, keepdims=True)) * (1.0 / HEAD_DIM)
        return x_t * lax.rsqrt(ms + NORM_EPS) * gain

    for p in range(N_PAIRS):
        cols = slice(p * PAIR, (p + 1) * PAIR)
        qn = normed(q_ref[:, cols].T, gq)
        kn = normed(k_ref[:, cols].T, gk)
        vt = v_ref[:, cols].T
        for hh in range(2):
            h = 2 * p + hh
            a1, a2, a3 = d1[h:h + 1], d2[h:h + 1], d3[h:h + 1]
            aq = jnp.where(r8 < 3, 1.0, jnp.where(r8 == 3, a1, jnp.where(r8 == 4, a2, jnp.where(r8 == 5, a3, 0.0))))
            ak = jnp.where(r8 == 0, -a1, jnp.where(r8 == 1, -a2, jnp.where(r8 == 2, -a3, jnp.where(r8 < 6, 1.0, 0.0))))

            def with_rows(x_t, extra, hh=hh):
                if hh == 0:
                    return jnp.concatenate([x_t[:HEAD_DIM], extra, fill], axis=0)
                return jnp.concatenate([extra, fill, x_t[HEAD_DIM:]], axis=0)

            qt_ref[0, h] = with_rows(qn, aq).astype(_BF16)
            ka_ref[0, h] = jnp.concatenate([with_rows(kn, ak), wide], axis=0).T.astype(_BF16)
            vt_ref[0, h] = with_rows(vt, ones_row).astype(_BF16)


def _fox_prep(pmat, qg2, kg2, bf_slot, batch, seq):
    tt = 256
    nt = seq // tt
    row_map = lambda c: (lambda b, t: (b * nt + t, c))
    row_sd = jax.ShapeDtypeStruct((batch, N_HEADS, seq, PAIR), _BF16)
    col_sd = jax.ShapeDtypeStruct((batch, N_HEADS, FOX_ROWS, seq), _BF16)
    row_spec = pl.BlockSpec((1, N_HEADS, tt, PAIR), lambda b, t: (b, 0, t, 0))
    col_spec = pl.BlockSpec((1, N_HEADS, FOX_ROWS, tt), lambda b, t: (b, 0, 0, t))
    vec = pl.BlockSpec((1, PAIR), lambda b, t: (0, 0))
    colvec = pl.BlockSpec((PAIR, 1), lambda b, t: (0, 0))
    return pl.pallas_call(
        functools.partial(_fox_prep_kernel, tt=tt),
        grid=(batch, nt),
        in_specs=[pl.BlockSpec((tt, D_HEADS), row_map(0)),
                  pl.BlockSpec((tt, D_HEADS), row_map(1)),
                  pl.BlockSpec((tt, D_HEADS), row_map(2)),
                  pl.BlockSpec((tt, SMALL_W), row_map(6 * D_HEADS // SMALL_W)),
                  colvec, colvec, vec],
        out_specs=[col_spec, row_spec, col_spec, pl.BlockSpec((tt, SMALL_W), row_map(0))],
        out_shape=[col_sd, row_sd, col_sd, jax.ShapeDtypeStruct((batch * seq, SMALL_W), _BF16)],
        scratch_shapes=[pltpu.VMEM((8, PAIR), _F32)],
        compiler_params=_cparams(("parallel", "arbitrary"), 40),
        name="fox_prep",
    )(pmat, pmat, pmat, pmat, qg2, kg2, bf_slot)


def _fox_attn_kernel(qt_ref, k_ref, vt_ref, g_ref, o_ref, m_ref, acc_ref, s_ref, *, tq, tk):
    qi = pl.program_id(2)
    key = lax.broadcasted_iota(jnp.int32, (tk, tq), 0)
    qry = lax.broadcasted_iota(jnp.int32, (tk, tq), 1)
    m_ref[...] = jnp.full(m_ref.shape, NEG_INF, _F32)
    acc_ref[...] = jnp.zeros(acc_ref.shape, _F32)
    pad = jnp.zeros((PAIR - FOX_ROWS, tq), _BF16)
    qt = [jnp.concatenate([qt_ref[0, hh], pad], axis=0) for hh in range(2)]

    def scores(blk, slot, hh, lanes=slice(None)):
        k = k_ref[0, hh, pl.ds(pl.multiple_of(blk * tk, tk), tk), :]
        s_ref[slot, hh, :, lanes] = jnp.dot(k, qt[hh][:, lanes], preferred_element_type=_F32)

    def absorb(blk, slot, hh, mask=None, lanes=slice(None)):
        vt = vt_ref[0, hh, :, pl.ds(pl.multiple_of(blk * tk, tk), tk)]
        st = s_ref[slot, hh, :, lanes]
        if mask is not None:
            st = jnp.where(mask, st, NEG_INF)
        m_old = m_ref[hh, :, lanes]
        m_new = jnp.maximum(m_old, jnp.max(st, axis=0, keepdims=True))
        p = jnp.exp2(st - m_new)
        acc_ref[hh, :, lanes] = jnp.exp2(m_old - m_new) * acc_ref[hh, :, lanes] + jnp.dot(
            vt, p.astype(_BF16), preferred_element_type=_F32)
        m_ref[hh, :, lanes] = m_new

    for hh in range(2):
        scores(0, 0, hh)

    def two_blocks(j2, carry):
        blk = 2 * j2
        for hh in range(2):
            scores(blk + 1, 1, hh)
            absorb(blk, 0, hh)
        for hh in range(2):
            scores(blk + 2, 0, hh)
            absorb(blk + 1, 1, hh)
        return carry

    lax.fori_loop(0, qi, two_blocks, 0)
    upper = slice(tk, tq)
    for hh in range(2):
        scores(2 * qi + 1, 1, hh, upper)
        absorb(2 * qi, 0, hh, key <= qry)
    for hh in range(2):
        absorb(2 * qi + 1, 1, hh, (key <= qry)[:, :tk], upper)

    acc0, acc1 = acc_ref[0], acc_ref[1]
    o0 = acc0[:HEAD_DIM] / acc0[HEAD_DIM:HEAD_DIM + 1]
    o1 = acc1[FOX_ROWS - HEAD_DIM:] / acc1[0:1]
    n0 = o0 * lax.rsqrt(jnp.mean(o0 * o0, axis=0, keepdims=True) + NORM_EPS)
    n1 = o1 * lax.rsqrt(jnp.mean(o1 * o1, axis=0, keepdims=True) + NORM_EPS)
    o_ref[0] = (jnp.concatenate([n0, n1], axis=0).T * g_ref[...]).astype(_BF16)


def _fox_attn(qt, ka, vt, og, batch, seq):
    tk = 512
    tq = 2 * tk
    nq = seq // tq
    return pl.pallas_call(
        functools.partial(_fox_attn_kernel, tq=tq, tk=tk),
        grid=(batch, N_PAIRS, nq),
        in_specs=[pl.BlockSpec((1, 2, FOX_ROWS, tq), lambda b, p, i: (b, p, 0, i)),
                  pl.BlockSpec((1, 2, seq, PAIR), lambda b, p, i: (b, p, 0, 0)),
                  pl.BlockSpec((1, 2, FOX_ROWS, seq), lambda b, p, i: (b, p, 0, 0)),
                  pl.BlockSpec((1, PAIR), lambda b, p, i: (0, p))],
        out_specs=pl.BlockSpec((1, tq, PAIR), lambda b, p, i: (b, i, p)),
        out_shape=jax.ShapeDtypeStruct((batch, seq, D_HEADS), _BF16),
        scratch_shapes=[pltpu.VMEM((2, 1, tq), _F32), pltpu.VMEM((2, FOX_ROWS, tq), _F32),
                        pltpu.VMEM((2, 2, tk, tq), _F32)],
        compiler_params=_cparams(("parallel", "parallel", "arbitrary"), 48),
        name="fox_attn",
    )(qt, ka, vt, og)


def _stack_heads(x, lo_half):
    return jnp.concatenate([jnp.where(lo_half, x, 0.0), jnp.where(lo_half, 0.0, x)], axis=0)


def _interleave(*gens):
    live = list(gens)
    while live:
        for gen in list(live):
            try:
                next(gen)
            except StopIteration:
                live.remove(gen)


def _rwkv_kernel(r_ref, k_ref, v_ref, lo_ref, w0_ref, a0_ref, kk_ref, ka_ref, rk_ref, lg_ref, lb_ref,
                 db_ref, ab_ref, gb_ref, o_ref, z_ref, lhs_ref, add_ref, dec_ref, bv_ref, gate_ref,
                 *, tt, nt):
    nc = tt // CHUNK
    c2 = 2 * CHUNK
    t = pl.program_id(2)
    wr = lax.rem(t, 2)
    rd = 1 - wr
    lo_half = _lo_half_mask()

    @pl.when(t == 0)
    def _():
        z_ref[...] = jnp.zeros_like(z_ref)
        lhs_ref[:, 1] = jnp.zeros((PAIRS_PER_STEP,) + lhs_ref.shape[2:], lhs_ref.dtype)
        add_ref[:, 1] = jnp.zeros((PAIRS_PER_STEP,) + add_ref.shape[2:], add_ref.dtype)
        dec_ref[:, 1] = jnp.ones((PAIRS_PER_STEP,) + dec_ref.shape[2:], dec_ref.dtype)
        bv_ref[:, 1] = jnp.zeros((PAIRS_PER_STEP,) + bv_ref.shape[2:], bv_ref.dtype)
        gate_ref[:, 1] = jnp.zeros((PAIRS_PER_STEP,) + gate_ref.shape[2:], gate_ref.dtype)

    def advance(q):
        lanes = slice(q * PAIR, (q + 1) * PAIR)
        z = z_ref[q]
        dec = dec_ref[q, rd]
        ys = []
        for c in range(nc):
            res = jnp.dot(lhs_ref[q, rd, c], z.astype(_BF16), preferred_element_type=_F32)
            add = add_ref[q, rd, c]
            ysh = res[:c2] + add[:c2]
            ys.append(ysh[:CHUNK] + ysh[CHUNK:])
            z = dec[:, c:c + 1] * z + res[c2:] + add[c2:]
            yield
        z_ref[q] = z
        y = jnp.concatenate(ys, axis=0)
        mu = _half_sum(y, lo_half) * (1.0 / HEAD_DIM)
        dlt = y - mu
        var = _half_sum(dlt * dlt, lo_half) * (1.0 / HEAD_DIM)
        yn = dlt * lax.rsqrt(var + LNX_EPS) * lg_ref[:, lanes] + lb_ref[:, lanes]
        o_ref[0, :, lanes] = ((yn + bv_ref[q, rd]) * gate_ref[q, rd]).astype(_BF16)
        yield

    def lora():
        w = -_softplus(-(w0_ref[...] + jnp.dot(lo_ref[:, 0:PAIR], db_ref[...],
                                                 preferred_element_type=_F32))) - 0.5
        logd = -jnp.exp(w)
        a = _sigmoid(a0_ref[...] + jnp.dot(lo_ref[:, PAIR:2 * PAIR], ab_ref[...],
                                           preferred_element_type=_F32))
        gate = jnp.dot(lo_ref[:, 2 * PAIR:], gb_ref[...], preferred_element_type=_F32)
        return logd, a, gate

    def prepare(q, logd_all, a_all, gate_all):
        lanes = slice(q * PAIR, (q + 1) * PAIR)
        r = r_ref[:, lanes]
        k = k_ref[:, lanes]
        v = v_ref[:, lanes]
        logd = logd_all[:, lanes]
        a = a_all[:, lanes]
        gate_ref[q, wr] = gate_all[:, lanes]
        kk = k * kk_ref[:, lanes]
        kk = kk * lax.rsqrt(jnp.maximum(_half_sum(kk * kk, lo_half), 1e-24))
        k = k * (1.0 + (a - 1.0) * ka_ref[:, lanes])
        bb = kk * a
        bv_ref[q, wr] = _half_sum(r * k * rk_ref[:, lanes], lo_half) * v
        yield

        pos = lax.broadcasted_iota(jnp.int32, (tt, 1), 0) & (CHUNK - 1)
        lcum = logd
        for j in range(CHUNK_LOG2):
            lcum = lcum + jnp.where(pos >= (1 << j), pltpu.roll(lcum, 1 << j, axis=0), 0.0)
        yield

        r2 = lax.broadcasted_iota(jnp.int32, (c2, c2), 0)
        q2 = lax.broadcasted_iota(jnp.int32, (c2, c2), 1)
        same = (r2 >> CHUNK_LOG2) == (q2 >> CHUNK_LOG2)
        strict = same & (r2 > q2)
        incl = same & (r2 >= q2)
        eye = jnp.where(r2 == q2, 1.0, 0.0)
        zeros_cc = jnp.zeros((c2, PAIR), _F32)

        rng = range(nc)
        sl = [slice(c * CHUNK, (c + 1) * CHUNK) for c in rng]
        lc = [lcum[sl[c]] for c in rng]
        lc_rows = [lc[c][CHUNK - 1:CHUNK, :] for c in rng]
        dec_ref[q, wr] = jnp.exp(jnp.concatenate(lc_rows + [jnp.zeros((PAIR - nc, PAIR), _F32)], axis=0).T)
        e_pos = [jnp.exp(lc[c]) for c in rng]
        e_neg = [jnp.exp(-lc[c]) for c in rng]
        e_end = [jnp.exp(lc_rows[c] - lc[c]) for c in rng]
        rp = [_stack_heads(r[sl[c]] * e_pos[c], lo_half) for c in rng]
        ap = [_stack_heads(-kk[sl[c]] * jnp.exp(lc[c] - logd[sl[c]]), lo_half) for c in rng]
        kq = [_stack_heads(k[sl[c]] * e_neg[c], lo_half) for c in rng]
        bq = [_stack_heads(bb[sl[c]] * e_neg[c], lo_half) for c in rng]
        kc = [_stack_heads(k[sl[c]] * e_end[c], lo_half) for c in rng]
        bc = [_stack_heads(bb[sl[c]] * e_end[c], lo_half) for c in rng]
        vs = [_stack_heads(v[sl[c]], lo_half) for c in rng]
        yield

        s = [_dot_nt(jnp.concatenate([ap[c], rp[c]], axis=0), jnp.concatenate([bq[c], kq[c]], axis=0))
             for c in rng]
        a_ab = [jnp.where(strict, s[c][:c2, :c2], 0.0) for c in rng]
        a_ak = [jnp.where(strict, s[c][:c2, c2:], 0.0) for c in rng]
        a_rb = [jnp.where(incl, s[c][c2:, :c2], 0.0) for c in rng]
        a_rk = [jnp.where(incl, s[c][c2:, c2:], 0.0) for c in rng]
        yield
        akv = [_dot(a_ak[c], vs[c]) for c in rng]

        pw = [_dot(a_ab[c], a_ab[c]) for c in rng]
        tinv = [eye + a_ab[c] for c in rng]
        yield
        for _ in range(CHUNK_LOG2 - 2):
            pt = [_dot(pw[c], jnp.concatenate([pw[c], tinv[c]], axis=1)) for c in rng]
            pw = [pt[c][:, :c2] for c in rng]
            tinv = [tinv[c] + pt[c][:, c2:] for c in rng]
            yield
        q5 = [_dot(tinv[c], jnp.concatenate([ap[c], akv[c]], axis=1)) for c in rng]
        yield
        wu = [q5[c] + _dot(pw[c], q5[c]) for c in rng]
        yield
        x = [jnp.concatenate([wu[c], jnp.concatenate([zeros_cc, vs[c]], axis=1)], axis=0) for c in rng]
        ry = [_dot(jnp.concatenate([a_rb[c], a_rk[c]], axis=1), x[c]) for c in rng]
        mz = [_dot_tn(jnp.concatenate([bc[c], kc[c]], axis=0), x[c]) for c in rng]
        yield
        for c in rng:
            lhs_ref[q, wr, c] = jnp.concatenate([rp[c] + ry[c][:, :PAIR], mz[c][:, :PAIR]], axis=0).astype(_BF16)
            add_ref[q, wr, c] = jnp.concatenate([ry[c][:, PAIR:], mz[c][:, PAIR:]], axis=0)
        yield

    pairs = range(PAIRS_PER_STEP)

    @pl.when(t < nt)
    def _():
        shared = lora()
        _interleave(*[prepare(q, *shared) for q in pairs], *[advance(q) for q in pairs])

    @pl.when(t == nt)
    def _():
        _interleave(*[advance(q) for q in pairs])


def _rwkv(pmat, lora_in, prm, batch, seq):
    tt = 512
    nt = seq // tt
    nc = tt // CHUNK
    npp = PAIRS_PER_STEP
    width = npp * PAIR
    cur = lambda b, t: b * nt + jnp.minimum(t, nt - 1)
    blk = lambda c0: pl.BlockSpec((tt, width), lambda b, p, t, c0=c0: (cur(b, t), c0 + p))
    vec = pl.BlockSpec((1, width), lambda b, p, t: (0, p))
    lora = lambda rows: pl.BlockSpec((rows, width), lambda b, p, t: (0, p))
    c_r = 3 * D_HEADS // width
    per = D_HEADS // width
    return pl.pallas_call(
        functools.partial(_rwkv_kernel, tt=tt, nt=nt),
        grid=(batch, N_PAIRS // npp, nt + 1),
        in_specs=[blk(c_r), blk(c_r + per), blk(c_r + 2 * per),
                  pl.BlockSpec((tt, SMALL_W), lambda b, p, t: (cur(b, t), 0)),
                  vec, vec, vec, vec, vec, vec, vec,
                  lora(PAIR), lora(PAIR), lora(GATE_LORA)],
        out_specs=pl.BlockSpec((1, tt, width), lambda b, p, t: (b, jnp.maximum(t - 1, 0), p)),
        out_shape=jax.ShapeDtypeStruct((batch, seq, D_HEADS), _BF16),
        scratch_shapes=[pltpu.VMEM((npp, PAIR, PAIR), _F32),
                        pltpu.VMEM((npp, 2, nc, 2 * PAIR, PAIR), _BF16),
                        pltpu.VMEM((npp, 2, nc, 2 * PAIR, PAIR), _F32),
                        pltpu.VMEM((npp, 2, PAIR, PAIR), _F32),
                        pltpu.VMEM((npp, 2, tt, PAIR), _F32),
                        pltpu.VMEM((npp, 2, tt, PAIR), _F32)],
        compiler_params=_cparams(("parallel", "parallel", "arbitrary"), 48),
        name="rwkv",
    )(pmat, pmat, pmat, lora_in, prm["w0"], prm["a0"], prm["k_k"], prm["k_a"], prm["r_k"],
      prm["lnx_g"], prm["lnx_b"], prm["decay_b"], prm["a_b"], prm["g_b"])


def _outproj_kernel(fox_ref, rw_ref, x_ref, w1_ref, w2_ref, gt_ref, g2_ref, sh_ref, sc_ref,
                    x1_ref, h2_ref):
    y = (jnp.dot(fox_ref[...], w1_ref[...], preferred_element_type=_F32)
         + jnp.dot(rw_ref[...], w2_ref[...], preferred_element_type=_F32))
    x1 = x_ref[...] + gt_ref[0] * y
    x1_ref[...] = x1
    h2_ref[...] = _adaln(x1, g2_ref, sh_ref, sc_ref)


def _outproj(fox, rw, x2, w1, w2, gt, g2, sh, sc, seq):
    m, d = x2.shape
    tm = 512
    per_b = seq // tm
    rows = lambda w: pl.BlockSpec((tm, w), lambda i: (i, 0))
    full = lambda a: pl.BlockSpec(a.shape, lambda i: (0, 0))
    mod = pl.BlockSpec((1, 1, d), lambda i: (i // per_b, 0, 0))
    return pl.pallas_call(
        _outproj_kernel,
        grid=(m // tm,),
        in_specs=[rows(D_HEADS), rows(D_HEADS), rows(d), full(w1), full(w2), mod, full(g2), mod, mod],
        out_specs=[rows(d), rows(d)],
        out_shape=[jax.ShapeDtypeStruct((m, d), _F32), jax.ShapeDtypeStruct((m, d), _BF16)],
        compiler_params=_cparams(("parallel",), 48),
        name="outproj",
    )(fox, rw, x2, w1, w2, gt, g2, sh, sc)


def _mlp_kernel(h_ref, wu_ref, wd_ref, x1_ref, gt_ref, o_ref):
    @pl.when(pl.program_id(1) == 0)
    def _():
        o_ref[...] = x1_ref[...]

    u = jnp.dot(h_ref[...], wu_ref[...], preferred_element_type=_F32)
    act = jnp.square(jnp.maximum(u, 0.0)).astype(_BF16)
    o_ref[...] += gt_ref[0] * jnp.dot(act, wd_ref[...], preferred_element_type=_F32)


def _mlp(h2, wu, wd, x1, gt, seq):
    m, d = x1.shape
    ff = wu.shape[1]
    tm, tf = 1024, 1024
    per_b = seq // tm
    rows = lambda **kw: pl.BlockSpec((tm, d), lambda i, f: (i, 0), **kw)
    return pl.pallas_call(
        _mlp_kernel,
        grid=(m // tm, ff // tf),
        in_specs=[rows(),
                  pl.BlockSpec((d, tf), lambda i, f: (0, f)),
                  pl.BlockSpec((tf, d), lambda i, f: (f, 0)),
                  rows(),
                  pl.BlockSpec((1, 1, d), lambda i, f: (i // per_b, 0, 0))],
        out_specs=rows(pipeline_mode=pl.Buffered(1)),
        out_shape=jax.ShapeDtypeStruct((m, d), _F32),
        compiler_params=_cparams(("parallel", "arbitrary"), 58),
        name="mlp",
    )(h2, wu, wd, x1, gt)


def _pad_rows(a, rows):
    return jnp.pad(a, ((0, rows - a.shape[0]), (0, 0)))


def _pad_lanes(a, width):
    return jnp.pad(a, ((0, 0), (0, width - a.shape[1])))


def _pack_kernel(wt_ref, o_ref):
    o_f = 3 * D_HEADS
    o_r = o_f + N_HEADS
    o_w = o_r + 3 * D_HEADS
    o_a = o_w + DECAY_LORA
    o_g = o_a + ICLR_LORA
    lanes = wt_ref.shape[1]
    dst = 0
    for src, size in ((0, o_f), (o_r, o_w - o_r), (o_w, DECAY_LORA), (o_f, N_HEADS),
                      (None, PAIR - DECAY_LORA - N_HEADS), (o_a, ICLR_LORA), (None, PAIR - ICLR_LORA),
                      (o_g, GATE_LORA)):
        if src is None:
            o_ref[dst:dst + size, :] = jnp.zeros((size, lanes), _BF16)
        else:
            o_ref[dst:dst + size, :] = wt_ref[src:src + size, :].astype(_BF16)
        dst += size


def _pack_w_in(w):
    wt = w.T
    n, d = wt.shape
    tl = 256
    return pl.pallas_call(
        _pack_kernel,
        grid=(d // tl,),
        in_specs=[pl.BlockSpec((n, tl), lambda i: (0, i))],
        out_specs=pl.BlockSpec((6 * D_HEADS + SMALL_W, tl), lambda i: (0, i)),
        out_shape=jax.ShapeDtypeStruct((6 * D_HEADS + SMALL_W, d), _BF16),
        compiler_params=_cparams(("parallel",), 40),
        name="pack_w_in",
    )(wt)


def _pack_shift_mu(mu):
    o_w = 3 * D_HEADS
    o_a = o_w + DECAY_LORA
    o_g = o_a + ICLR_LORA
    return jnp.concatenate([jnp.zeros((1, 3 * D_HEADS), _F32), mu[:, :o_w], _pad_lanes(mu[:, o_w:o_a], PAIR),
                            _pad_lanes(mu[:, o_a:o_g], PAIR), mu[:, o_g:]], axis=1)


def kernel(x, c, w_ada, b_ada, norm1_g, w_in, b_forget, q_norm_g, k_norm_g, fox_out_g, shift_mu, w0, decay_b, a0, a_b, g_b, k_k, k_a, r_k, lnx_g, lnx_b, w_out, norm2_g, w_mlp_up, w_mlp_down):
    batch, seq, d = x.shape
    depth = w_ada.shape[0]
    x2 = x.reshape(batch * seq, d)
    row = lambda a: a.reshape(1, -1)
    for l in range(depth):
        mod = _ada(c, w_ada[l], b_ada[l])
        sh1, sc1, gt1, sh2, sc2, gt2 = [m.reshape(batch, 1, d) for m in jnp.split(mod, 6, axis=-1)]

        pmat = _proj(x2, row(norm1_g[l]), sh1, sc1, _pack_w_in(w_in[l]), _pack_shift_mu(row(shift_mu[l])), seq)

        bf_slot = jnp.pad(row(b_forget[l]), ((0, 0), (F_LANE0, PAIR - F_LANE0 - N_HEADS)))
        qt, ka, vt, lora_in = _fox_prep(pmat, jnp.tile(q_norm_g[l], 2).reshape(PAIR, 1),
                                        jnp.tile(k_norm_g[l], 2).reshape(PAIR, 1), bf_slot, batch, seq)
        fox = _fox_attn(qt, ka, vt, row(fox_out_g[l]), batch, seq)

        prm = {
            "w0": row(w0[l]), "a0": row(a0[l]), "k_k": row(k_k[l]), "k_a": row(k_a[l]),
            "r_k": row(r_k[l]), "lnx_g": row(lnx_g[l]), "lnx_b": row(lnx_b[l]),
            "decay_b": _pad_rows(decay_b[l], PAIR).astype(_BF16),
            "a_b": _pad_rows(a_b[l], PAIR).astype(_BF16),
            "g_b": g_b[l].astype(_BF16),
        }
        rw = _rwkv(pmat, lora_in, prm, batch, seq)

        w_o = w_out[l].astype(_BF16)
        x1, h2 = _outproj(fox.reshape(batch * seq, D_HEADS), rw.reshape(batch * seq, D_HEADS), x2,
                          w_o[:D_HEADS], w_o[D_HEADS:], gt1, row(norm2_g[l]), sh2, sc2, seq)
        x2 = _mlp(h2, w_mlp_up[l].astype(_BF16), w_mlp_down[l].astype(_BF16), x1, gt2, seq)
    return x2.reshape(batch, seq, d)
```

```python
import functools

import jax
import jax.numpy as jnp
from jax import lax
from jax.experimental import pallas as pl
from jax.experimental.pallas import tpu as pltpu

_F32 = jnp.float32
_BF16 = jnp.bfloat16

HEAD_DIM = 64
PAIR = 2 * HEAD_DIM
N_HEADS = 16
D_HEADS = N_HEADS * HEAD_DIM
N_PAIRS = N_HEADS // 2
DECAY_LORA = 96
ICLR_LORA = 96
GATE_LORA = 256
SMALL_W = 512
F_LANE0 = DECAY_LORA
NORM_EPS = 1e-6
LNX_EPS = 64e-5
NEG_INF = -1e30
LOG2E = 1.4426950408889634
CHUNK_LOG2 = 6
FOX_ROWS = HEAD_DIM + 16
PAIRS_PER_STEP = 4
CHUNK = 1 << CHUNK_LOG2
MIB = 1024 * 1024


def _cparams(sem, vmem_mib):
    return pltpu.CompilerParams(dimension_semantics=sem, vmem_limit_bytes=vmem_mib * MIB)


def _dot(a, b):
    return jnp.dot(a.astype(_BF16), b.astype(_BF16), preferred_element_type=_F32)


def _dot_nt(a, b):
    return lax.dot_general(a.astype(_BF16), b.astype(_BF16), (((1,), (1,)), ((), ())),
                           preferred_element_type=_F32)


def _dot_tn(a, b):
    return lax.dot_general(a.astype(_BF16), b.astype(_BF16), (((0,), (0,)), ((), ())),
                           preferred_element_type=_F32)


def _split3(x):
    h = x.astype(_BF16).astype(_F32)
    r = x - h
    m = r.astype(_BF16).astype(_F32)
    l = (r - m).astype(_BF16).astype(_F32)
    return h, m, l


def _dot_exact_rhs(a_bf16, x):
    return sum(jnp.dot(a_bf16, p.astype(_BF16), preferred_element_type=_F32) for p in _split3(x))


def _sigmoid(x):
    return 1.0 / (1.0 + jnp.exp(-x))


def _softplus(x):
    return jnp.maximum(x, 0.0) + jnp.log(1.0 + jnp.exp(-jnp.abs(x)))


def _half_sum(x, lo_half):
    s_lo = jnp.sum(jnp.where(lo_half, x, 0.0), axis=-1, keepdims=True)
    s_hi = jnp.sum(jnp.where(lo_half, 0.0, x), axis=-1, keepdims=True)
    return jnp.where(lo_half, s_lo, s_hi)


def _lo_half_mask():
    return lax.broadcasted_iota(jnp.int32, (1, PAIR), 1) < HEAD_DIM


def _ada_kernel(c_ref, w_ref, b_ref, o_ref):
    c = c_ref[...]
    cond = c * _sigmoid(c)
    o_ref[...] = _dot(cond, w_ref[...]) + b_ref[...]


def _ada(c, w_ada, b_ada):
    b, d = c.shape
    n = w_ada.shape[1]
    rows = 8 * pl.cdiv(b, 8)
    tn = 1024
    cp = jnp.pad(c, ((0, rows - b), (0, 0)))
    out = pl.pallas_call(
        _ada_kernel,
        grid=(n // tn,),
        in_specs=[pl.BlockSpec((rows, d), lambda j: (0, 0)),
                  pl.BlockSpec((d, tn), lambda j: (0, j)),
                  pl.BlockSpec((1, tn), lambda j: (0, j))],
        out_specs=pl.BlockSpec((rows, tn), lambda j: (0, j)),
        out_shape=jax.ShapeDtypeStruct((rows, n), _F32),
        compiler_params=_cparams(("parallel",), 40),
        name="ada",
    )(cp, w_ada, b_ada.reshape(1, n))
    return out[:b]


NORM_ROWS = 128


def _adaln(x, g_ref, sh_ref, sc_ref):
    y = x * lax.rsqrt(jnp.mean(x * x, axis=-1, keepdims=True) + NORM_EPS)
    return (y * (g_ref[...] * (1.0 + sc_ref[0])) + sh_ref[0]).astype(_BF16)


def _proj_kernel(x_ref, g_ref, sh_ref, sc_ref, w_ref, mu_ref, fox_ref, rw_ref, h_ref, carry_ref,
                 *, tm, per_b):
    i = pl.program_id(0)
    j = pl.program_id(1)

    @pl.when(j == 0)
    def _():
        for c in range(tm // NORM_ROWS):
            rows = slice(c * NORM_ROWS, (c + 1) * NORM_ROWS)
            h_ref[rows, :] = _adaln(x_ref[rows, :], g_ref, sh_ref, sc_ref)

    y = lax.dot_general(h_ref[...], w_ref[...], (((1,), (1,)), ((), ())), preferred_element_type=_F32)
    first_row = lax.broadcasted_iota(jnp.int32, (tm, 1), 0) == 0
    above = jnp.where(lax.rem(i, per_b) == 0, 0.0, carry_ref[j, 0:1, :])
    prev = jnp.where(first_row, above, pltpu.roll(y, 1, axis=0))
    carry_ref[j, 0:1, :] = y[tm - 1:tm, :]
    fox_ref[...] = y.astype(_BF16)
    rw_ref[...] = y + (prev - y) * mu_ref[...]


def _proj(x2, g, sh, sc, wt, mu, seq):
    m, d = x2.shape
    n = wt.shape[0]
    tm, tn = 1024, 512
    per_b = seq // tm
    fox_tiles = 3 * D_HEADS // tn
    rw_tiles = n // tn - fox_tiles
    return pl.pallas_call(
        functools.partial(_proj_kernel, tm=tm, per_b=per_b),
        grid=(m // tm, n // tn),
        in_specs=[pl.BlockSpec((tm, d), lambda i, j: (i, 0)),
                  pl.BlockSpec((1, d), lambda i, j: (0, 0)),
                  pl.BlockSpec((1, 1, d), lambda i, j: (i // per_b, 0, 0)),
                  pl.BlockSpec((1, 1, d), lambda i, j: (i // per_b, 0, 0)),
                  pl.BlockSpec((tn, d), lambda i, j: (j, 0)),
                  pl.BlockSpec((1, tn), lambda i, j: (0, j))],
        out_specs=[pl.BlockSpec((tm, tn), lambda i, j: (i, jnp.minimum(j, fox_tiles))),
                   pl.BlockSpec((tm, tn), lambda i, j: (i, jnp.where(j < fox_tiles, rw_tiles, j - fox_tiles)))],
        out_shape=[jax.ShapeDtypeStruct((m, (fox_tiles + 1) * tn), _BF16),
                   jax.ShapeDtypeStruct((m, (rw_tiles + 1) * tn), _F32)],
        scratch_shapes=[pltpu.VMEM((tm, d), _BF16), pltpu.VMEM((n // tn, 8, tn), _F32)],
        compiler_params=_cparams(("arbitrary", "arbitrary"), 48),
        name="proj",
    )(x2, g, sh, sc, wt, mu)


def _fox_prep_kernel(q_ref, k_ref, v_ref, s_ref, qg_ref, kg_ref, bf_ref, qt_ref, ka_ref, vt_ref,
                     lo_ref, carry_ref, *, tt):
    @pl.when(pl.program_id(1) == 0)
    def _():
        carry_ref[...] = jnp.zeros_like(carry_ref)

    lo_ref[:, 0:PAIR] = jnp.tanh(s_ref[:, 0:PAIR]).astype(_BF16)
    lo_ref[:, PAIR:2 * PAIR] = s_ref[:, PAIR:2 * PAIR].astype(_BF16)
    lo_ref[:, 2 * PAIR:] = _sigmoid(s_ref[:, 2 * PAIR:]).astype(_BF16)

    z = s_ref[:, 0:PAIR] + bf_ref[...]
    logf = -_softplus(-z)
    row = lax.broadcasted_iota(jnp.int32, (tt, tt), 0)
    col = lax.broadcasted_iota(jnp.int32, (tt, tt), 1)
    tri = jnp.where(row >= col, 1.0, 0.0).astype(_BF16)
    dcum = carry_ref[0:1, :] + _dot_exact_rhs(tri, logf)
    carry_ref[0:1, :] = dcum[tt - 1:tt, :]
    d1, d2, d3 = _split3((dcum * LOG2E).T[F_LANE0:F_LANE0 + N_HEADS, :])

    r8 = lax.broadcasted_iota(jnp.int32, (8, tt), 0)
    lo_rows = lax.broadcasted_iota(jnp.int32, (PAIR, 1), 0) < HEAD_DIM
    ones_row = jnp.where(r8 == 0, 1.0, 0.0)
    fill = jnp.zeros((FOX_ROWS - HEAD_DIM - 8, tt), _F32)
    wide = jnp.zeros((PAIR - FOX_ROWS, tt), _F32)
    gq = jnp.broadcast_to(qg_ref[...] * (HEAD_DIM ** -0.5 * LOG2E), (PAIR, tt))
    gk = jnp.broadcast_to(kg_ref[...], (PAIR, tt))

    def normed(x_t, gain):
        sq = x_t * x_t
        ms = jnp.where(lo_rows, jnp.sum(sq[:HEAD_DIM], axis=0, keepdims=True),
                       jnp.sum(sq[HEAD_DIM:], axis=0, keepdims=True)) * (1.0 / HEAD_DIM)
        return x_t * lax.rsqrt(ms + NORM_EPS) * gain

    for p in range(N_PAIRS):
        cols = slice(p * PAIR, (p + 1) * PAIR)
        qn = normed(q_ref[:, cols].astype(_F32).T, gq)
        kn = normed(k_ref[:, cols].astype(_F32).T, gk)
        vt = v_ref[:, cols].astype(_F32).T
        for hh in range(2):
            h = 2 * p + hh
            a1, a2, a3 = d1[h:h + 1], d2[h:h + 1], d3[h:h + 1]
            aq = jnp.where(r8 < 3, 1.0, jnp.where(r8 == 3, a1, jnp.where(r8 == 4, a2, jnp.where(r8 == 5, a3, 0.0))))
            ak = jnp.where(r8 == 0, -a1, jnp.where(r8 == 1, -a2, jnp.where(r8 == 2, -a3, jnp.where(r8 < 6, 1.0, 0.0))))

            def with_rows(x_t, extra, hh=hh):
                if hh == 0:
                    return jnp.concatenate([x_t[:HEAD_DIM], extra, fill], axis=0)
                return jnp.concatenate([extra, fill, x_t[HEAD_DIM:]], axis=0)

            qt_ref[0, h] = with_rows(qn, aq).astype(_BF16)
            ka_ref[0, h] = jnp.concatenate([with_rows(kn, ak), wide], axis=0).T.astype(_BF16)
            vt_ref[0, h] = with_rows(vt, ones_row).astype(_BF16)


def _fox_prep(pfox, prw, qg2, kg2, bf_slot, batch, seq):
    tt = 512
    nt = seq // tt
    row_map = lambda c: (lambda b, t: (b * nt + t, c))
    row_sd = jax.ShapeDtypeStruct((batch, N_HEADS, seq, PAIR), _BF16)
    col_sd = jax.ShapeDtypeStruct((batch, N_HEADS, FOX_ROWS, seq), _BF16)
    row_spec = pl.BlockSpec((1, N_HEADS, tt, PAIR), lambda b, t: (b, 0, t, 0))
    col_spec = pl.BlockSpec((1, N_HEADS, FOX_ROWS, tt), lambda b, t: (b, 0, 0, t))
    vec = pl.BlockSpec((1, PAIR), lambda b, t: (0, 0))
    colvec = pl.BlockSpec((PAIR, 1), lambda b, t: (0, 0))
    return pl.pallas_call(
        functools.partial(_fox_prep_kernel, tt=tt),
        grid=(batch, nt),
        in_specs=[pl.BlockSpec((tt, D_HEADS), row_map(0)),
                  pl.BlockSpec((tt, D_HEADS), row_map(1)),
                  pl.BlockSpec((tt, D_HEADS), row_map(2)),
                  pl.BlockSpec((tt, SMALL_W), row_map(3 * D_HEADS // SMALL_W)),
                  colvec, colvec, vec],
        out_specs=[col_spec, row_spec, col_spec, pl.BlockSpec((tt, SMALL_W), row_map(0))],
        out_shape=[col_sd, row_sd, col_sd, jax.ShapeDtypeStruct((batch * seq, SMALL_W), _BF16)],
        scratch_shapes=[pltpu.VMEM((8, PAIR), _F32)],
        compiler_params=_cparams(("parallel", "arbitrary"), 40),
        name="fox_prep",
    )(pfox, pfox, pfox, prw, qg2, kg2, bf_slot)


def _fox_attn_kernel(qt_ref, k_ref, vt_ref, g_ref, o_ref, m_ref, acc_ref, s_ref, *, tq, tk):
    qi = pl.program_id(2)
    key = lax.broadcasted_iota(jnp.int32, (tk, tq), 0)
    qry = lax.broadcasted_iota(jnp.int32, (tk, tq), 1)
    m_ref[...] = jnp.full(m_ref.shape, NEG_INF, _F32)
    acc_ref[...] = jnp.zeros(acc_ref.shape, _F32)
    pad = jnp.zeros((PAIR - FOX_ROWS, tq), _BF16)
    qt = [jnp.concatenate([qt_ref[0, hh], pad], axis=0) for hh in range(2)]

    def scores(blk, slot, hh, lanes=slice(None)):
        k = k_ref[0, hh, pl.ds(pl.multiple_of(blk * tk, tk), tk), :]
        s_ref[slot, hh, :, lanes] = jnp.dot(k, qt[hh][:, lanes], preferred_element_type=_F32)

    def absorb(blk, slot, hh, mask=None, lanes=slice(None)):
        vt = vt_ref[0, hh, :, pl.ds(pl.multiple_of(blk * tk, tk), tk)]
        st = s_ref[slot, hh, :, lanes]
        if mask is not None:
            st = jnp.where(mask, st, NEG_INF)
        m_old = m_ref[hh, :, lanes]
        m_new = jnp.maximum(m_old, jnp.max(st, axis=0, keepdims=True))
        p = jnp.exp2(st - m_new)
        acc_ref[hh, :, lanes] = jnp.exp2(m_old - m_new) * acc_ref[hh, :, lanes] + jnp.dot(
            vt, p.astype(_BF16), preferred_element_type=_F32)
        m_ref[hh, :, lanes] = m_new

    for hh in range(2):
        scores(0, 0, hh)

    def two_blocks(j2, carry):
        blk = 2 * j2
        for hh in range(2):
            scores(blk + 1, 1, hh)
            absorb(blk, 0, hh)
        for hh in range(2):
            scores(blk + 2, 0, hh)
            absorb(blk + 1, 1, hh)
        return carry

    lax.fori_loop(0, qi, two_blocks, 0)
    upper = slice(tk, tq)
    for hh in range(2):
        scores(2 * qi + 1, 1, hh, upper)
        absorb(2 * qi, 0, hh, key <= qry)
    for hh in range(2):
        absorb(2 * qi + 1, 1, hh, (key <= qry)[:, :tk], upper)

    acc0, acc1 = acc_ref[0], acc_ref[1]
    o0 = acc0[:HEAD_DIM] / acc0[HEAD_DIM:HEAD_DIM + 1]
    o1 = acc1[FOX_ROWS - HEAD_DIM:] / acc1[0:1]
    n0 = o0 * lax.rsqrt(jnp.mean(o0 * o0, axis=0, keepdims=True) + NORM_EPS)
    n1 = o1 * lax.rsqrt(jnp.mean(o1 * o1, axis=0, keepdims=True) + NORM_EPS)
    o_ref[0] = (jnp.concatenate([n0, n1], axis=0).T * g_ref[...]).astype(_BF16)


def _fox_attn(qt, ka, vt, og, batch, seq):
    tk = 512
    tq = 2 * tk
    nq = seq // tq
    return pl.pallas_call(
        functools.partial(_fox_attn_kernel, tq=tq, tk=tk),
        grid=(batch, N_PAIRS, nq),
        in_specs=[pl.BlockSpec((1, 2, FOX_ROWS, tq), lambda b, p, i: (b, p, 0, i)),
                  pl.BlockSpec((1, 2, seq, PAIR), lambda b, p, i: (b, p, 0, 0)),
                  pl.BlockSpec((1, 2, FOX_ROWS, seq), lambda b, p, i: (b, p, 0, 0)),
                  pl.BlockSpec((1, PAIR), lambda b, p, i: (0, p))],
        out_specs=pl.BlockSpec((1, tq, PAIR), lambda b, p, i: (b, i, p)),
        out_shape=jax.ShapeDtypeStruct((batch, seq, D_HEADS), _BF16),
        scratch_shapes=[pltpu.VMEM((2, 1, tq), _F32), pltpu.VMEM((2, FOX_ROWS, tq), _F32),
                        pltpu.VMEM((2, 2, tk, tq), _F32)],
        compiler_params=_cparams(("parallel", "parallel", "arbitrary"), 48),
        name="fox_attn",
    )(qt, ka, vt, og)


def _stack_heads(x, lo_half):
    return jnp.concatenate([jnp.where(lo_half, x, 0.0), jnp.where(lo_half, 0.0, x)], axis=0)


def _interleave(*gens):
    live = list(gens)
    while live:
        for gen in list(live):
            try:
                next(gen)
            except StopIteration:
                live.remove(gen)


def _rwkv_kernel(r_ref, k_ref, v_ref, lo_ref, w0_ref, a0_ref, kk_ref, ka_ref, rk_ref, lg_ref, lb_ref,
                 db_ref, ab_ref, gb_ref, o_ref, z_ref, lhs_ref, add_ref, dec_ref, bv_ref, gate_ref,
                 *, tt, nt):
    nc = tt // CHUNK
    c2 = 2 * CHUNK
    t = pl.program_id(2)
    wr = lax.rem(t, 2)
    rd = 1 - wr
    lo_half = _lo_half_mask()

    @pl.when(t == 0)
    def _():
        z_ref[...] = jnp.zeros_like(z_ref)
        lhs_ref[:, 1] = jnp.zeros((PAIRS_PER_STEP,) + lhs_ref.shape[2:], lhs_ref.dtype)
        add_ref[:, 1] = jnp.zeros((PAIRS_PER_STEP,) + add_ref.shape[2:], add_ref.dtype)
        dec_ref[:, 1] = jnp.ones((PAIRS_PER_STEP,) + dec_ref.shape[2:], dec_ref.dtype)
        bv_ref[:, 1] = jnp.zeros((PAIRS_PER_STEP,) + bv_ref.shape[2:], bv_ref.dtype)
        gate_ref[:, 1] = jnp.zeros((PAIRS_PER_STEP,) + gate_ref.shape[2:], gate_ref.dtype)

    def advance(q):
        lanes = slice(q * PAIR, (q + 1) * PAIR)
        z = z_ref[q]
        dec = dec_ref[q, rd]
        ys = []
        for c in range(nc):
            res = jnp.dot(lhs_ref[q, rd, c], z.astype(_BF16), preferred_element_type=_F32)
            add = add_ref[q, rd, c]
            ysh = res[:c2] + add[:c2]
            ys.append(ysh[:CHUNK] + ysh[CHUNK:])
            z = dec[:, c:c + 1] * z + res[c2:] + add[c2:]
            yield
        z_ref[q] = z
        y = jnp.concatenate(ys, axis=0)
        mu = _half_sum(y, lo_half) * (1.0 / HEAD_DIM)
        dlt = y - mu
        var = _half_sum(dlt * dlt, lo_half) * (1.0 / HEAD_DIM)
        yn = dlt * lax.rsqrt(var + LNX_EPS) * lg_ref[:, lanes] + lb_ref[:, lanes]
        o_ref[0, :, lanes] = ((yn + bv_ref[q, rd]) * gate_ref[q, rd]).astype(_BF16)
        yield

    def lora():
        w = -_softplus(-(w0_ref[...] + jnp.dot(lo_ref[:, 0:PAIR], db_ref[...],
                                                 preferred_element_type=_F32))) - 0.5
        logd = -jnp.exp(w)
        a = _sigmoid(a0_ref[...] + jnp.dot(lo_ref[:, PAIR:2 * PAIR], ab_ref[...],
                                           preferred_element_type=_F32))
        gate = jnp.dot(lo_ref[:, 2 * PAIR:], gb_ref[...], preferred_element_type=_F32)
        return logd, a, gate

    def prepare(q, logd_all, a_all, gate_all):
        lanes = slice(q * PAIR, (q + 1) * PAIR)
        r = r_ref[:, lanes]
        k = k_ref[:, lanes]
        v = v_ref[:, lanes]
        logd = logd_all[:, lanes]
        a = a_all[:, lanes]
        gate_ref[q, wr] = gate_all[:, lanes]
        kk = k * kk_ref[:, lanes]
        kk = kk * lax.rsqrt(jnp.maximum(_half_sum(kk * kk, lo_half), 1e-24))
        k = k * (1.0 + (a - 1.0) * ka_ref[:, lanes])
        bb = kk * a
        bv_ref[q, wr] = _half_sum(r * k * rk_ref[:, lanes], lo_half) * v
        yield

        pos = lax.broadcasted_iota(jnp.int32, (tt, 1), 0) & (CHUNK - 1)
        lcum = logd
        for j in range(CHUNK_LOG2):
            lcum = lcum + jnp.where(pos >= (1 << j), pltpu.roll(lcum, 1 << j, axis=0), 0.0)
        yield

        r2 = lax.broadcasted_iota(jnp.int32, (c2, c2), 0)
        q2 = lax.broadcasted_iota(jnp.int32, (c2, c2), 1)
        same = (r2 >> CHUNK_LOG2) == (q2 >> CHUNK_LOG2)
        strict = same & (r2 > q2)
        incl = same & (r2 >= q2)
        eye = jnp.where(r2 == q2, 1.0, 0.0)
        zeros_cc = jnp.zeros((c2, PAIR), _F32)

        rng = range(nc)
        sl = [slice(c * CHUNK, (c + 1) * CHUNK) for c in rng]
        lc = [lcum[sl[c]] for c in rng]
        lc_rows = [lc[c][CHUNK - 1:CHUNK, :] for c in rng]
        dec_ref[q, wr] = jnp.exp(jnp.concatenate(lc_rows + [jnp.zeros((PAIR - nc, PAIR), _F32)], axis=0).T)
        e_pos = [jnp.exp(lc[c]) for c in rng]
        e_neg = [jnp.exp(-lc[c]) for c in rng]
        e_end = [jnp.exp(lc_rows[c] - lc[c]) for c in rng]
        rp = [_stack_heads(r[sl[c]] * e_pos[c], lo_half) for c in rng]
        ap = [_stack_heads(-kk[sl[c]] * jnp.exp(lc[c] - logd[sl[c]]), lo_half) for c in rng]
        kq = [_stack_heads(k[sl[c]] * e_neg[c], lo_half) for c in rng]
        bq = [_stack_heads(bb[sl[c]] * e_neg[c], lo_half) for c in rng]
        kc = [_stack_heads(k[sl[c]] * e_end[c], lo_half) for c in rng]
        bc = [_stack_heads(bb[sl[c]] * e_end[c], lo_half) for c in rng]
        vs = [_stack_heads(v[sl[c]], lo_half) for c in rng]
        yield

        s = [_dot_nt(jnp.concatenate([ap[c], rp[c]], axis=0), jnp.concatenate([bq[c], kq[c]], axis=0))
             for c in rng]
        a_ab = [jnp.where(strict, s[c][:c2, :c2], 0.0) for c in rng]
        a_ak = [jnp.where(strict, s[c][:c2, c2:], 0.0) for c in rng]
        a_rb = [jnp.where(incl, s[c][c2:, :c2], 0.0) for c in rng]
        a_rk = [jnp.where(incl, s[c][c2:, c2:], 0.0) for c in rng]
        yield
        akv = [_dot(a_ak[c], vs[c]) for c in rng]

        pw = [_dot(a_ab[c], a_ab[c]) for c in rng]
        tinv = [eye + a_ab[c] for c in rng]
        yield
        for _ in range(CHUNK_LOG2 - 2):
            pt = [_dot(pw[c], jnp.concatenate([pw[c], tinv[c]], axis=1)) for c in rng]
            pw = [pt[c][:, :c2] for c in rng]
            tinv = [tinv[c] + pt[c][:, c2:] for c in rng]
            yield
        q5 = [_dot(tinv[c], jnp.concatenate([ap[c], akv[c]], axis=1)) for c in rng]
        yield
        wu = [q5[c] + _dot(pw[c], q5[c]) for c in rng]
        yield
        x = [jnp.concatenate([wu[c], jnp.concatenate([zeros_cc, vs[c]], axis=1)], axis=0) for c in rng]
        ry = [_dot(jnp.concatenate([a_rb[c], a_rk[c]], axis=1), x[c]) for c in rng]
        mz = [_dot_tn(jnp.concatenate([bc[c], kc[c]], axis=0), x[c]) for c in rng]
        yield
        for c in rng:
            lhs_ref[q, wr, c] = jnp.concatenate([rp[c] + ry[c][:, :PAIR], mz[c][:, :PAIR]], axis=0).astype(_BF16)
            add_ref[q, wr, c] = jnp.concatenate([ry[c][:, PAIR:], mz[c][:, PAIR:]], axis=0)
        yield

    pairs = range(PAIRS_PER_STEP)

    @pl.when(t < nt)
    def _():
        shared = lora()
        _interleave(*[prepare(q, *shared) for q in pairs], *[advance(q) for q in pairs])

    @pl.when(t == nt)
    def _():
        _interleave(*[advance(q) for q in pairs])


def _rwkv(prw, lora_in, prm, batch, seq):
    tt = 512
    nt = seq // tt
    nc = tt // CHUNK
    npp = PAIRS_PER_STEP
    width = npp * PAIR
    cur = lambda b, t: b * nt + jnp.minimum(t, nt - 1)
    blk = lambda c0: pl.BlockSpec((tt, width), lambda b, p, t, c0=c0: (cur(b, t), c0 + p))
    vec = pl.BlockSpec((1, width), lambda b, p, t: (0, p))
    lora = lambda rows: pl.BlockSpec((rows, width), lambda b, p, t: (0, p))
    per = D_HEADS // width
    return pl.pallas_call(
        functools.partial(_rwkv_kernel, tt=tt, nt=nt),
        grid=(batch, N_PAIRS // npp, nt + 1),
        in_specs=[blk(0), blk(per), blk(2 * per),
                  pl.BlockSpec((tt, SMALL_W), lambda b, p, t: (cur(b, t), 0)),
                  vec, vec, vec, vec, vec, vec, vec,
                  lora(PAIR), lora(PAIR), lora(GATE_LORA)],
        out_specs=pl.BlockSpec((1, tt, width), lambda b, p, t: (b, jnp.maximum(t - 1, 0), p)),
        out_shape=jax.ShapeDtypeStruct((batch, seq, D_HEADS), _BF16),
        scratch_shapes=[pltpu.VMEM((npp, PAIR, PAIR), _F32),
                        pltpu.VMEM((npp, 2, nc, 2 * PAIR, PAIR), _BF16),
                        pltpu.VMEM((npp, 2, nc, 2 * PAIR, PAIR), _F32),
                        pltpu.VMEM((npp, 2, PAIR, PAIR), _F32),
                        pltpu.VMEM((npp, 2, tt, PAIR), _F32),
                        pltpu.VMEM((npp, 2, tt, PAIR), _F32)],
        compiler_params=_cparams(("parallel", "parallel", "arbitrary"), 48),
        name="rwkv",
    )(prw, prw, prw, lora_in, prm["w0"], prm["a0"], prm["k_k"], prm["k_a"], prm["r_k"],
      prm["lnx_g"], prm["lnx_b"], prm["decay_b"], prm["a_b"], prm["g_b"])


def _outproj_kernel(fox_ref, rw_ref, x_ref, w1_ref, w2_ref, gt_ref, g2_ref, sh_ref, sc_ref,
                    x1_ref, h2_ref):
    y = (jnp.dot(fox_ref[...], w1_ref[...], preferred_element_type=_F32)
         + jnp.dot(rw_ref[...], w2_ref[...], preferred_element_type=_F32))
    x1 = x_ref[...] + gt_ref[0] * y
    x1_ref[...] = x1
    h2_ref[...] = _adaln(x1, g2_ref, sh_ref, sc_ref)


def _outproj(fox, rw, x2, w1, w2, gt, g2, sh, sc, seq):
    m, d = x2.shape
    tm = 512
    per_b = seq // tm
    rows = lambda w: pl.BlockSpec((tm, w), lambda i: (i, 0))
    full = lambda a: pl.BlockSpec(a.shape, lambda i: (0, 0))
    mod = pl.BlockSpec((1, 1, d), lambda i: (i // per_b, 0, 0))
    return pl.pallas_call(
        _outproj_kernel,
        grid=(m // tm,),
        in_specs=[rows(D_HEADS), rows(D_HEADS), rows(d), full(w1), full(w2), mod, full(g2), mod, mod],
        out_specs=[rows(d), rows(d)],
        out_shape=[jax.ShapeDtypeStruct((m, d), _F32), jax.ShapeDtypeStruct((m, d), _BF16)],
        compiler_params=_cparams(("parallel",), 48),
        name="outproj",
    )(fox, rw, x2, w1, w2, gt, g2, sh, sc)


def _mlp_kernel(h_ref, wu_ref, wd_ref, x1_ref, gt_ref, o_ref):
    @pl.when(pl.program_id(1) == 0)
    def _():
        o_ref[...] = x1_ref[...]

    u = jnp.dot(h_ref[...], wu_ref[...], preferred_element_type=_F32)
    act = jnp.square(jnp.maximum(u, 0.0)).astype(_BF16)
    o_ref[...] += gt_ref[0] * jnp.dot(act, wd_ref[...], preferred_element_type=_F32)


def _mlp(h2, wu, wd, x1, gt, seq):
    m, d = x1.shape
    ff = wu.shape[1]
    tm, tf = 512, 2048
    per_b = seq // tm
    rows = lambda **kw: pl.BlockSpec((tm, d), lambda i, f: (i, 0), **kw)
    return pl.pallas_call(
        _mlp_kernel,
        grid=(m // tm, ff // tf),
        in_specs=[rows(),
                  pl.BlockSpec((d, tf), lambda i, f: (0, f)),
                  pl.BlockSpec((tf, d), lambda i, f: (f, 0)),
                  rows(),
                  pl.BlockSpec((1, 1, d), lambda i, f: (i // per_b, 0, 0))],
        out_specs=rows(),
        out_shape=jax.ShapeDtypeStruct((m, d), _F32),
        compiler_params=_cparams(("parallel", "arbitrary"), 58),
        name="mlp",
    )(h2, wu, wd, x1, gt)


def _pad_rows(a, rows):
    return jnp.pad(a, ((0, rows - a.shape[0]), (0, 0)))


def _pad_lanes(a, width):
    return jnp.pad(a, ((0, 0), (0, width - a.shape[1])))


def _pack_kernel(wt_ref, o_ref):
    o_f = 3 * D_HEADS
    o_r = o_f + N_HEADS
    o_w = o_r + 3 * D_HEADS
    o_a = o_w + DECAY_LORA
    o_g = o_a + ICLR_LORA
    lanes = wt_ref.shape[1]
    dst = 0
    for src, size in ((0, o_f), (o_r, o_w - o_r), (o_w, DECAY_LORA), (o_f, N_HEADS),
                      (None, PAIR - DECAY_LORA - N_HEADS), (o_a, ICLR_LORA), (None, PAIR - ICLR_LORA),
                      (o_g, GATE_LORA)):
        if src is None:
            o_ref[dst:dst + size, :] = jnp.zeros((size, lanes), _BF16)
        else:
            o_ref[dst:dst + size, :] = wt_ref[src:src + size, :].astype(_BF16)
        dst += size


def _pack_w_in(w):
    wt = w.T
    n, d = wt.shape
    tl = 256
    return pl.pallas_call(
        _pack_kernel,
        grid=(d // tl,),
        in_specs=[pl.BlockSpec((n, tl), lambda i: (0, i))],
        out_specs=pl.BlockSpec((6 * D_HEADS + SMALL_W, tl), lambda i: (0, i)),
        out_shape=jax.ShapeDtypeStruct((6 * D_HEADS + SMALL_W, d), _BF16),
        compiler_params=_cparams(("parallel",), 40),
        name="pack_w_in",
    )(wt)


def _pack_shift_mu(mu):
    o_w = 3 * D_HEADS
    o_a = o_w + DECAY_LORA
    o_g = o_a + ICLR_LORA
    return jnp.concatenate([jnp.zeros((1, 3 * D_HEADS), _F32), mu[:, :o_w], _pad_lanes(mu[:, o_w:o_a], PAIR),
                            _pad_lanes(mu[:, o_a:o_g], PAIR), mu[:, o_g:]], axis=1)


def kernel(x, c, w_ada, b_ada, norm1_g, w_in, b_forget, q_norm_g, k_norm_g, fox_out_g, shift_mu, w0, decay_b, a0, a_b, g_b, k_k, k_a, r_k, lnx_g, lnx_b, w_out, norm2_g, w_mlp_up, w_mlp_down):
    batch, seq, d = x.shape
    depth = w_ada.shape[0]
    x2 = x.reshape(batch * seq, d)
    row = lambda a: a.reshape(1, -1)
    for l in range(depth):
        mod = _ada(c, w_ada[l], b_ada[l])
        sh1, sc1, gt1, sh2, sc2, gt2 = [m.reshape(batch, 1, d) for m in jnp.split(mod, 6, axis=-1)]

        pfox, prw = _proj(x2, row(norm1_g[l]), sh1, sc1, _pack_w_in(w_in[l]), _pack_shift_mu(row(shift_mu[l])), seq)

        bf_slot = jnp.pad(row(b_forget[l]), ((0, 0), (F_LANE0, PAIR - F_LANE0 - N_HEADS)))
        qt, ka, vt, lora_in = _fox_prep(pfox, prw, jnp.tile(q_norm_g[l], 2).reshape(PAIR, 1),
                                        jnp.tile(k_norm_g[l], 2).reshape(PAIR, 1), bf_slot, batch, seq)
        fox = _fox_attn(qt, ka, vt, row(fox_out_g[l]), batch, seq)

        prm = {
            "w0": row(w0[l]), "a0": row(a0[l]), "k_k": row(k_k[l]), "k_a": row(k_a[l]),
            "r_k": row(r_k[l]), "lnx_g": row(lnx_g[l]), "lnx_b": row(lnx_b[l]),
            "decay_b": _pad_rows(decay_b[l], PAIR).astype(_BF16),
            "a_b": _pad_rows(a_b[l], PAIR).astype(_BF16),
            "g_b": g_b[l].astype(_BF16),
        }
        rw = _rwkv(prw, lora_in, prm, batch, seq)

        w_o = w_out[l].astype(_BF16)
        x1, h2 = _outproj(fox.reshape(batch * seq, D_HEADS), rw.reshape(batch * seq, D_HEADS), x2,
                          w_o[:D_HEADS], w_o[D_HEADS:], gt1, row(norm2_g[l]), sh2, sc2, seq)
        x2 = _mlp(h2, w_mlp_up[l].astype(_BF16), w_mlp_down[l].astype(_BF16), x1, gt2, seq)
    return x2.reshape(batch, seq, d)
```

```python
import functools

import jax
import jax.numpy as jnp
from jax import lax
from jax.experimental import pallas as pl
from jax.experimental.pallas import tpu as pltpu

_F32 = jnp.float32
_BF16 = jnp.bfloat16

HEAD_DIM = 64
PAIR = 2 * HEAD_DIM
N_HEADS = 16
D_HEADS = N_HEADS * HEAD_DIM
N_PAIRS = N_HEADS // 2
DECAY_LORA = 96
ICLR_LORA = 96
GATE_LORA = 256
SMALL_W = 512
F_LANE0 = DECAY_LORA
NORM_EPS = 1e-6
LNX_EPS = 64e-5
NEG_INF = -1e30
LOG2E = 1.4426950408889634
CHUNK_LOG2 = 6
FOX_ROWS = HEAD_DIM + 16
PAIRS_PER_STEP = 4
CHUNK = 1 << CHUNK_LOG2
MIB = 1024 * 1024

ADA_COLS, ADA_VMEM = 1024, 40
PACK_LANES, PACK_VMEM = 256, 40
PROJ_ROWS, PROJ_COLS, PROJ_VMEM = 1024, 512, 48
FOX_PREP_ROWS, FOX_PREP_VMEM = 512, 40
FOX_KEYS, FOX_VMEM = 512, 48
RWKV_ROWS, RWKV_VMEM = 512, 48
OUTPROJ_ROWS, OUTPROJ_VMEM = 512, 48
MLP_ROWS, MLP_HIDDEN, MLP_VMEM = 512, 2048, 58


def _cparams(sem, vmem_mib):
    return pltpu.CompilerParams(dimension_semantics=sem, vmem_limit_bytes=vmem_mib * MIB)


def _dot(a, b):
    return jnp.dot(a.astype(_BF16), b.astype(_BF16), preferred_element_type=_F32)


def _dot_nt(a, b):
    return lax.dot_general(a.astype(_BF16), b.astype(_BF16), (((1,), (1,)), ((), ())),
                           preferred_element_type=_F32)


def _dot_tn(a, b):
    return lax.dot_general(a.astype(_BF16), b.astype(_BF16), (((0,), (0,)), ((), ())),
                           preferred_element_type=_F32)


def _split3(x):
    h = x.astype(_BF16).astype(_F32)
    r = x - h
    m = r.astype(_BF16).astype(_F32)
    l = (r - m).astype(_BF16).astype(_F32)
    return h, m, l


def _dot_exact_rhs(a_bf16, x):
    return sum(jnp.dot(a_bf16, p.astype(_BF16), preferred_element_type=_F32) for p in _split3(x))


def _sigmoid(x):
    return 1.0 / (1.0 + jnp.exp(-x))


def _softplus(x):
    return jnp.maximum(x, 0.0) + jnp.log(1.0 + jnp.exp(-jnp.abs(x)))


def _half_sum(x, lo_half):
    s_lo = jnp.sum(jnp.where(lo_half, x, 0.0), axis=-1, keepdims=True)
    s_hi = jnp.sum(jnp.where(lo_half, 0.0, x), axis=-1, keepdims=True)
    return jnp.where(lo_half, s_lo, s_hi)


def _lo_half_mask():
    return lax.broadcasted_iota(jnp.int32, (1, PAIR), 1) < HEAD_DIM


def _ada_kernel(c_ref, w_ref, b_ref, o_ref):
    c = c_ref[...]
    cond = c * _sigmoid(c)
    o_ref[...] = _dot(cond, w_ref[...]) + b_ref[...]


def _ada(c, w_ada, b_ada):
    b, d = c.shape
    n = w_ada.shape[1]
    rows = 8 * pl.cdiv(b, 8)
    tn = ADA_COLS
    cp = jnp.pad(c, ((0, rows - b), (0, 0)))
    out = pl.pallas_call(
        _ada_kernel,
        grid=(n // tn,),
        in_specs=[pl.BlockSpec((rows, d), lambda j: (0, 0)),
                  pl.BlockSpec((d, tn), lambda j: (0, j)),
                  pl.BlockSpec((1, tn), lambda j: (0, j))],
        out_specs=pl.BlockSpec((rows, tn), lambda j: (0, j)),
        out_shape=jax.ShapeDtypeStruct((rows, n), _F32),
        compiler_params=_cparams(("parallel",), ADA_VMEM),
        name="ada",
    )(cp, w_ada, b_ada.reshape(1, n))
    return out[:b]


NORM_ROWS = 128


def _adaln(x, g_ref, sh_ref, sc_ref):
    y = x * lax.rsqrt(jnp.mean(x * x, axis=-1, keepdims=True) + NORM_EPS)
    return (y * (g_ref[...] * (1.0 + sc_ref[0])) + sh_ref[0]).astype(_BF16)


def _proj_kernel(x_ref, g_ref, sh_ref, sc_ref, w_ref, mu_ref, fox_ref, rw_ref, h_ref, carry_ref,
                 *, tm, per_b):
    i = pl.program_id(0)
    j = pl.program_id(1)

    @pl.when(j == 0)
    def _():
        for c in range(tm // NORM_ROWS):
            rows = slice(c * NORM_ROWS, (c + 1) * NORM_ROWS)
            h_ref[rows, :] = _adaln(x_ref[rows, :], g_ref, sh_ref, sc_ref)

    y = lax.dot_general(h_ref[...], w_ref[...], (((1,), (1,)), ((), ())), preferred_element_type=_F32)
    first_row = lax.broadcasted_iota(jnp.int32, (tm, 1), 0) == 0
    above = jnp.where(lax.rem(i, per_b) == 0, 0.0, carry_ref[j, 0:1, :])
    prev = jnp.where(first_row, above, pltpu.roll(y, 1, axis=0))
    carry_ref[j, 0:1, :] = y[tm - 1:tm, :]
    fox_ref[...] = y.astype(_BF16)
    rw_ref[...] = y + (prev - y) * mu_ref[...]


def _proj(x2, g, sh, sc, wt, mu, seq):
    m, d = x2.shape
    n = wt.shape[0]
    tm, tn = PROJ_ROWS, PROJ_COLS
    per_b = seq // tm
    fox_tiles = 3 * D_HEADS // tn
    rw_tiles = n // tn - fox_tiles
    return pl.pallas_call(
        functools.partial(_proj_kernel, tm=tm, per_b=per_b),
        grid=(m // tm, n // tn),
        in_specs=[pl.BlockSpec((tm, d), lambda i, j: (i, 0)),
                  pl.BlockSpec((1, d), lambda i, j: (0, 0)),
                  pl.BlockSpec((1, 1, d), lambda i, j: (i // per_b, 0, 0)),
                  pl.BlockSpec((1, 1, d), lambda i, j: (i // per_b, 0, 0)),
                  pl.BlockSpec((tn, d), lambda i, j: (j, 0)),
                  pl.BlockSpec((1, tn), lambda i, j: (0, j))],
        out_specs=[pl.BlockSpec((tm, tn), lambda i, j: (i, jnp.minimum(j, fox_tiles))),
                   pl.BlockSpec((tm, tn), lambda i, j: (i, jnp.where(j < fox_tiles, rw_tiles, j - fox_tiles)))],
        out_shape=[jax.ShapeDtypeStruct((m, (fox_tiles + 1) * tn), _BF16),
                   jax.ShapeDtypeStruct((m, (rw_tiles + 1) * tn), _F32)],
        scratch_shapes=[pltpu.VMEM((tm, d), _BF16), pltpu.VMEM((n // tn, 8, tn), _F32)],
        compiler_params=_cparams(("arbitrary", "arbitrary"), PROJ_VMEM),
        name="proj",
    )(x2, g, sh, sc, wt, mu)


def _fox_prep_kernel(q_ref, k_ref, v_ref, s_ref, qg_ref, kg_ref, bf_ref, qt_ref, ka_ref, vt_ref,
                     lo_ref, carry_ref, *, tt):
    @pl.when(pl.program_id(1) == 0)
    def _():
        carry_ref[...] = jnp.zeros_like(carry_ref)

    lo_ref[:, 0:PAIR] = jnp.tanh(s_ref[:, 0:PAIR]).astype(_BF16)
    lo_ref[:, PAIR:2 * PAIR] = s_ref[:, PAIR:2 * PAIR].astype(_BF16)
    lo_ref[:, 2 * PAIR:] = _sigmoid(s_ref[:, 2 * PAIR:]).astype(_BF16)

    z = s_ref[:, 0:PAIR] + bf_ref[...]
    logf = -_softplus(-z)
    row = lax.broadcasted_iota(jnp.int32, (tt, tt), 0)
    col = lax.broadcasted_iota(jnp.int32, (tt, tt), 1)
    tri = jnp.where(row >= col, 1.0, 0.0).astype(_BF16)
    dcum = carry_ref[0:1, :] + _dot_exact_rhs(tri, logf)
    carry_ref[0:1, :] = dcum[tt - 1:tt, :]
    d1, d2, d3 = _split3((dcum * LOG2E).T[F_LANE0:F_LANE0 + N_HEADS, :])

    r8 = lax.broadcasted_iota(jnp.int32, (8, tt), 0)
    lo_rows = lax.broadcasted_iota(jnp.int32, (PAIR, 1), 0) < HEAD_DIM
    ones_row = jnp.where(r8 == 0, 1.0, 0.0)
    fill = jnp.zeros((FOX_ROWS - HEAD_DIM - 8, tt), _F32)
    wide = jnp.zeros((PAIR - FOX_ROWS, tt), _F32)
    gq = jnp.broadcast_to(qg_ref[...] * (HEAD_DIM ** -0.5 * LOG2E), (PAIR, tt))
    gk = jnp.broadcast_to(kg_ref[...], (PAIR, tt))

    def normed(x_t, gain):
        sq = x_t * x_t
        ms = jnp.where(lo_rows, jnp.sum(sq[:HEAD_DIM], axis=0, keepdims=True),
                       jnp.sum(sq[HEAD_DIM:], axis=0, keepdims=True)) * (1.0 / HEAD_DIM)
        return x_t * lax.rsqrt(ms + NORM_EPS) * gain

    for p in range(N_PAIRS):
        cols = slice(p * PAIR, (p + 1) * PAIR)
        qn = normed(q_ref[:, cols].astype(_F32).T, gq)
        kn = normed(k_ref[:, cols].astype(_F32).T, gk)
        vt = v_ref[:, cols].astype(_F32).T
        for hh in range(2):
            h = 2 * p + hh
            a1, a2, a3 = d1[h:h + 1], d2[h:h + 1], d3[h:h + 1]
            aq = jnp.where(r8 < 3, 1.0, jnp.where(r8 == 3, a1, jnp.where(r8 == 4, a2, jnp.where(r8 == 5, a3, 0.0))))
            ak = jnp.where(r8 == 0, -a1, jnp.where(r8 == 1, -a2, jnp.where(r8 == 2, -a3, jnp.where(r8 < 6, 1.0, 0.0))))

            def with_rows(x_t, extra, hh=hh):
                if hh == 0:
                    return jnp.concatenate([x_t[:HEAD_DIM], extra, fill], axis=0)
                return jnp.concatenate([extra, fill, x_t[HEAD_DIM:]], axis=0)

            qt_ref[0, h] = with_rows(qn, aq).astype(_BF16)
            ka_ref[0, h] = jnp.concatenate([with_rows(kn, ak), wide], axis=0).T.astype(_BF16)
            vt_ref[0, h] = with_rows(vt, ones_row).astype(_BF16)


def _fox_prep(pfox, prw, qg2, kg2, bf_slot, batch, seq):
    tt = FOX_PREP_ROWS
    nt = seq // tt
    row_map = lambda c: (lambda b, t: (b * nt + t, c))
    row_sd = jax.ShapeDtypeStruct((batch, N_HEADS, seq, PAIR), _BF16)
    col_sd = jax.ShapeDtypeStruct((batch, N_HEADS, FOX_ROWS, seq), _BF16)
    row_spec = pl.BlockSpec((1, N_HEADS, tt, PAIR), lambda b, t: (b, 0, t, 0))
    col_spec = pl.BlockSpec((1, N_HEADS, FOX_ROWS, tt), lambda b, t: (b, 0, 0, t))
    vec = pl.BlockSpec((1, PAIR), lambda b, t: (0, 0))
    colvec = pl.BlockSpec((PAIR, 1), lambda b, t: (0, 0))
    return pl.pallas_call(
        functools.partial(_fox_prep_kernel, tt=tt),
        grid=(batch, nt),
        in_specs=[pl.BlockSpec((tt, D_HEADS), row_map(0)),
                  pl.BlockSpec((tt, D_HEADS), row_map(1)),
                  pl.BlockSpec((tt, D_HEADS), row_map(2)),
                  pl.BlockSpec((tt, SMALL_W), row_map(3 * D_HEADS // SMALL_W)),
                  colvec, colvec, vec],
        out_specs=[col_spec, row_spec, col_spec, pl.BlockSpec((tt, SMALL_W), row_map(0))],
        out_shape=[col_sd, row_sd, col_sd, jax.ShapeDtypeStruct((batch * seq, SMALL_W), _BF16)],
        scratch_shapes=[pltpu.VMEM((8, PAIR), _F32)],
        compiler_params=_cparams(("parallel", "arbitrary"), FOX_PREP_VMEM),
        name="fox_prep",
    )(pfox, pfox, pfox, prw, qg2, kg2, bf_slot)


def _fox_attn_kernel(qt_ref, k_ref, vt_ref, g_ref, o_ref, m_ref, acc_ref, s_ref, *, tq, tk):
    qi = pl.program_id(2)
    key = lax.broadcasted_iota(jnp.int32, (tk, tq), 0)
    qry = lax.broadcasted_iota(jnp.int32, (tk, tq), 1)
    m_ref[...] = jnp.full(m_ref.shape, NEG_INF, _F32)
    acc_ref[...] = jnp.zeros(acc_ref.shape, _F32)
    pad = jnp.zeros((PAIR - FOX_ROWS, tq), _BF16)
    qt = [jnp.concatenate([qt_ref[0, hh], pad], axis=0) for hh in range(2)]

    def scores(blk, slot, hh, lanes=slice(None)):
        k = k_ref[0, hh, pl.ds(pl.multiple_of(blk * tk, tk), tk), :]
        s_ref[slot, hh, :, lanes] = jnp.dot(k, qt[hh][:, lanes], preferred_element_type=_F32)

    def absorb(blk, slot, hh, mask=None, lanes=slice(None)):
        vt = vt_ref[0, hh, :, pl.ds(pl.multiple_of(blk * tk, tk), tk)]
        st = s_ref[slot, hh, :, lanes]
        if mask is not None:
            st = jnp.where(mask, st, NEG_INF)
        m_old = m_ref[hh, :, lanes]
        m_new = jnp.maximum(m_old, jnp.max(st, axis=0, keepdims=True))
        p = jnp.exp2(st - m_new)
        acc_ref[hh, :, lanes] = jnp.exp2(m_old - m_new) * acc_ref[hh, :, lanes] + jnp.dot(
            vt, p.astype(_BF16), preferred_element_type=_F32)
        m_ref[hh, :, lanes] = m_new

    for hh in range(2):
        scores(0, 0, hh)

    def two_blocks(j2, carry):
        blk = 2 * j2
        for hh in range(2):
            scores(blk + 1, 1, hh)
            absorb(blk, 0, hh)
        for hh in range(2):
            scores(blk + 2, 0, hh)
            absorb(blk + 1, 1, hh)
        return carry

    lax.fori_loop(0, qi, two_blocks, 0)
    upper = slice(tk, tq)
    for hh in range(2):
        scores(2 * qi + 1, 1, hh, upper)
        absorb(2 * qi, 0, hh, key <= qry)
    for hh in range(2):
        absorb(2 * qi + 1, 1, hh, (key <= qry)[:, :tk], upper)

    acc0, acc1 = acc_ref[0], acc_ref[1]
    o0 = acc0[:HEAD_DIM] / acc0[HEAD_DIM:HEAD_DIM + 1]
    o1 = acc1[FOX_ROWS - HEAD_DIM:] / acc1[0:1]
    n0 = o0 * lax.rsqrt(jnp.mean(o0 * o0, axis=0, keepdims=True) + NORM_EPS)
    n1 = o1 * lax.rsqrt(jnp.mean(o1 * o1, axis=0, keepdims=True) + NORM_EPS)
    o_ref[0] = (jnp.concatenate([n0, n1], axis=0).T * g_ref[...]).astype(_BF16)


def _fox_attn(qt, ka, vt, og, batch, seq):
    tk = FOX_KEYS
    tq = 2 * tk
    nq = seq // tq
    return pl.pallas_call(
        functools.partial(_fox_attn_kernel, tq=tq, tk=tk),
        grid=(batch, N_PAIRS, nq),
        in_specs=[pl.BlockSpec((1, 2, FOX_ROWS, tq), lambda b, p, i: (b, p, 0, i)),
                  pl.BlockSpec((1, 2, seq, PAIR), lambda b, p, i: (b, p, 0, 0)),
                  pl.BlockSpec((1, 2, FOX_ROWS, seq), lambda b, p, i: (b, p, 0, 0)),
                  pl.BlockSpec((1, PAIR), lambda b, p, i: (0, p))],
        out_specs=pl.BlockSpec((1, tq, PAIR), lambda b, p, i: (b, i, p)),
        out_shape=jax.ShapeDtypeStruct((batch, seq, D_HEADS), _BF16),
        scratch_shapes=[pltpu.VMEM((2, 1, tq), _F32), pltpu.VMEM((2, FOX_ROWS, tq), _F32),
                        pltpu.VMEM((2, 2, tk, tq), _F32)],
        compiler_params=_cparams(("parallel", "parallel", "arbitrary"), FOX_VMEM),
        name="fox_attn",
    )(qt, ka, vt, og)


def _stack_heads(x, lo_half):
    return jnp.concatenate([jnp.where(lo_half, x, 0.0), jnp.where(lo_half, 0.0, x)], axis=0)


def _interleave(*gens):
    live = list(gens)
    while live:
        for gen in list(live):
            try:
                next(gen)
            except StopIteration:
                live.remove(gen)


def _rwkv_kernel(r_ref, k_ref, v_ref, lo_ref, w0_ref, a0_ref, kk_ref, ka_ref, rk_ref, lg_ref, lb_ref,
                 db_ref, ab_ref, gb_ref, o_ref, z_ref, lhs_ref, add_ref, dec_ref, bv_ref, gate_ref,
                 *, tt, nt):
    nc = tt // CHUNK
    c2 = 2 * CHUNK
    t = pl.program_id(2)
    wr = lax.rem(t, 2)
    rd = 1 - wr
    lo_half = _lo_half_mask()

    @pl.when(t == 0)
    def _():
        z_ref[...] = jnp.zeros_like(z_ref)
        lhs_ref[:, 1] = jnp.zeros((PAIRS_PER_STEP,) + lhs_ref.shape[2:], lhs_ref.dtype)
        add_ref[:, 1] = jnp.zeros((PAIRS_PER_STEP,) + add_ref.shape[2:], add_ref.dtype)
        dec_ref[:, 1] = jnp.ones((PAIRS_PER_STEP,) + dec_ref.shape[2:], dec_ref.dtype)
        bv_ref[:, 1] = jnp.zeros((PAIRS_PER_STEP,) + bv_ref.shape[2:], bv_ref.dtype)
        gate_ref[:, 1] = jnp.zeros((PAIRS_PER_STEP,) + gate_ref.shape[2:], gate_ref.dtype)

    def advance(q):
        lanes = slice(q * PAIR, (q + 1) * PAIR)
        z = z_ref[q]
        dec = dec_ref[q, rd]
        ys = []
        for c in range(nc):
            res = jnp.dot(lhs_ref[q, rd, c], z.astype(_BF16), preferred_element_type=_F32)
            add = add_ref[q, rd, c]
            ysh = res[:c2] + add[:c2]
            ys.append(ysh[:CHUNK] + ysh[CHUNK:])
            z = dec[:, c:c + 1] * z + res[c2:] + add[c2:]
            yield
        z_ref[q] = z
        y = jnp.concatenate(ys, axis=0)
        mu = _half_sum(y, lo_half) * (1.0 / HEAD_DIM)
        dlt = y - mu
        var = _half_sum(dlt * dlt, lo_half) * (1.0 / HEAD_DIM)
        yn = dlt * lax.rsqrt(var + LNX_EPS) * lg_ref[:, lanes] + lb_ref[:, lanes]
        o_ref[0, :, lanes] = ((yn + bv_ref[q, rd]) * gate_ref[q, rd]).astype(_BF16)
        yield

    def lora():
        w = -_softplus(-(w0_ref[...] + jnp.dot(lo_ref[:, 0:PAIR], db_ref[...],
                                                 preferred_element_type=_F32))) - 0.5
        logd = -jnp.exp(w)
        a = _sigmoid(a0_ref[...] + jnp.dot(lo_ref[:, PAIR:2 * PAIR], ab_ref[...],
                                           preferred_element_type=_F32))
        gate = jnp.dot(lo_ref[:, 2 * PAIR:], gb_ref[...], preferred_element_type=_F32)
        return logd, a, gate

    def prepare(q, logd_all, a_all, gate_all):
        lanes = slice(q * PAIR, (q + 1) * PAIR)
        r = r_ref[:, lanes]
        k = k_ref[:, lanes]
        v = v_ref[:, lanes]
        logd = logd_all[:, lanes]
        a = a_all[:, lanes]
        gate_ref[q, wr] = gate_all[:, lanes]
        kk = k * kk_ref[:, lanes]
        kk = kk * lax.rsqrt(jnp.maximum(_half_sum(kk * kk, lo_half), 1e-24))
        k = k * (1.0 + (a - 1.0) * ka_ref[:, lanes])
        bb = kk * a
        bv_ref[q, wr] = _half_sum(r * k * rk_ref[:, lanes], lo_half) * v
        yield

        pos = lax.broadcasted_iota(jnp.int32, (tt, 1), 0) & (CHUNK - 1)
        lcum = logd
        for j in range(CHUNK_LOG2):
            lcum = lcum + jnp.where(pos >= (1 << j), pltpu.roll(lcum, 1 << j, axis=0), 0.0)
        yield

        r2 = lax.broadcasted_iota(jnp.int32, (c2, c2), 0)
        q2 = lax.broadcasted_iota(jnp.int32, (c2, c2), 1)
        same = (r2 >> CHUNK_LOG2) == (q2 >> CHUNK_LOG2)
        strict = same & (r2 > q2)
        incl = same & (r2 >= q2)
        eye = jnp.where(r2 == q2, 1.0, 0.0)
        zeros_cc = jnp.zeros((c2, PAIR), _F32)

        rng = range(nc)
        sl = [slice(c * CHUNK, (c + 1) * CHUNK) for c in rng]
        lc = [lcum[sl[c]] for c in rng]
        lc_rows = [lc[c][CHUNK - 1:CHUNK, :] for c in rng]
        dec_ref[q, wr] = jnp.exp(jnp.concatenate(lc_rows + [jnp.zeros((PAIR - nc, PAIR), _F32)], axis=0).T)
        e_pos = [jnp.exp(lc[c]) for c in rng]
        e_neg = [jnp.exp(-lc[c]) for c in rng]
        e_end = [jnp.exp(lc_rows[c] - lc[c]) for c in rng]
        rp = [_stack_heads(r[sl[c]] * e_pos[c], lo_half) for c in rng]
        ap = [_stack_heads(-kk[sl[c]] * jnp.exp(lc[c] - logd[sl[c]]), lo_half) for c in rng]
        kq = [_stack_heads(k[sl[c]] * e_neg[c], lo_half) for c in rng]
        bq = [_stack_heads(bb[sl[c]] * e_neg[c], lo_half) for c in rng]
        kc = [_stack_heads(k[sl[c]] * e_end[c], lo_half) for c in rng]
        bc = [_stack_heads(bb[sl[c]] * e_end[c], lo_half) for c in rng]
        vs = [_stack_heads(v[sl[c]], lo_half) for c in rng]
        yield

        s = [_dot_nt(jnp.concatenate([ap[c], rp[c]], axis=0), jnp.concatenate([bq[c], kq[c]], axis=0))
             for c in rng]
        a_ab = [jnp.where(strict, s[c][:c2, :c2], 0.0) for c in rng]
        a_ak = [jnp.where(strict, s[c][:c2, c2:], 0.0) for c in rng]
        a_rb = [jnp.where(incl, s[c][c2:, :c2], 0.0) for c in rng]
        a_rk = [jnp.where(incl, s[c][c2:, c2:], 0.0) for c in rng]
        yield
        akv = [_dot(a_ak[c], vs[c]) for c in rng]

        pw = [_dot(a_ab[c], a_ab[c]) for c in rng]
        tinv = [eye + a_ab[c] for c in rng]
        yield
        for _ in range(CHUNK_LOG2 - 2):
            pt = [_dot(pw[c], jnp.concatenate([pw[c], tinv[c]], axis=1)) for c in rng]
            pw = [pt[c][:, :c2] for c in rng]
            tinv = [tinv[c] + pt[c][:, c2:] for c in rng]
            yield
        q5 = [_dot(tinv[c], jnp.concatenate([ap[c], akv[c]], axis=1)) for c in rng]
        yield
        wu = [q5[c] + _dot(pw[c], q5[c]) for c in rng]
        yield
        x = [jnp.concatenate([wu[c], jnp.concatenate([zeros_cc, vs[c]], axis=1)], axis=0) for c in rng]
        ry = [_dot(jnp.concatenate([a_rb[c], a_rk[c]], axis=1), x[c]) for c in rng]
        mz = [_dot_tn(jnp.concatenate([bc[c], kc[c]], axis=0), x[c]) for c in rng]
        yield
        for c in rng:
            lhs_ref[q, wr, c] = jnp.concatenate([rp[c] + ry[c][:, :PAIR], mz[c][:, :PAIR]], axis=0).astype(_BF16)
            add_ref[q, wr, c] = jnp.concatenate([ry[c][:, PAIR:], mz[c][:, PAIR:]], axis=0)
        yield

    pairs = range(PAIRS_PER_STEP)

    @pl.when(t < nt)
    def _():
        shared = lora()
        _interleave(*[prepare(q, *shared) for q in pairs], *[advance(q) for q in pairs])

    @pl.when(t == nt)
    def _():
        _interleave(*[advance(q) for q in pairs])


def _rwkv(prw, lora_in, prm, batch, seq):
    tt = RWKV_ROWS
    nt = seq // tt
    nc = tt // CHUNK
    npp = PAIRS_PER_STEP
    width = npp * PAIR
    cur = lambda b, t: b * nt + jnp.minimum(t, nt - 1)
    blk = lambda c0: pl.BlockSpec((tt, width), lambda b, p, t, c0=c0: (cur(b, t), c0 + p))
    vec = pl.BlockSpec((1, width), lambda b, p, t: (0, p))
    lora = lambda rows: pl.BlockSpec((rows, width), lambda b, p, t: (0, p))
    per = D_HEADS // width
    return pl.pallas_call(
        functools.partial(_rwkv_kernel, tt=tt, nt=nt),
        grid=(batch, N_PAIRS // npp, nt + 1),
        in_specs=[blk(0), blk(per), blk(2 * per),
                  pl.BlockSpec((tt, SMALL_W), lambda b, p, t: (cur(b, t), 0)),
                  vec, vec, vec, vec, vec, vec, vec,
                  lora(PAIR), lora(PAIR), lora(GATE_LORA)],
        out_specs=pl.BlockSpec((1, tt, width), lambda b, p, t: (b, jnp.maximum(t - 1, 0), p)),
        out_shape=jax.ShapeDtypeStruct((batch, seq, D_HEADS), _BF16),
        scratch_shapes=[pltpu.VMEM((npp, PAIR, PAIR), _F32),
                        pltpu.VMEM((npp, 2, nc, 2 * PAIR, PAIR), _BF16),
                        pltpu.VMEM((npp, 2, nc, 2 * PAIR, PAIR), _F32),
                        pltpu.VMEM((npp, 2, PAIR, PAIR), _F32),
                        pltpu.VMEM((npp, 2, tt, PAIR), _F32),
                        pltpu.VMEM((npp, 2, tt, PAIR), _F32)],
        compiler_params=_cparams(("parallel", "parallel", "arbitrary"), RWKV_VMEM),
        name="rwkv",
    )(prw, prw, prw, lora_in, prm["w0"], prm["a0"], prm["k_k"], prm["k_a"], prm["r_k"],
      prm["lnx_g"], prm["lnx_b"], prm["decay_b"], prm["a_b"], prm["g_b"])


def _outproj_kernel(fox_ref, rw_ref, x_ref, w1_ref, w2_ref, gt_ref, g2_ref, sh_ref, sc_ref,
                    x1_ref, h2_ref):
    y = (jnp.dot(fox_ref[...], w1_ref[...], preferred_element_type=_F32)
         + jnp.dot(rw_ref[...], w2_ref[...], preferred_element_type=_F32))
    x1 = x_ref[...] + gt_ref[0] * y
    x1_ref[...] = x1
    h2_ref[...] = _adaln(x1, g2_ref, sh_ref, sc_ref)


def _outproj(fox, rw, x2, w1, w2, gt, g2, sh, sc, seq):
    m, d = x2.shape
    tm = OUTPROJ_ROWS
    per_b = seq // tm
    rows = lambda w: pl.BlockSpec((tm, w), lambda i: (i, 0))
    full = lambda a: pl.BlockSpec(a.shape, lambda i: (0, 0))
    mod = pl.BlockSpec((1, 1, d), lambda i: (i // per_b, 0, 0))
    return pl.pallas_call(
        _outproj_kernel,
        grid=(m // tm,),
        in_specs=[rows(D_HEADS), rows(D_HEADS), rows(d), full(w1), full(w2), mod, full(g2), mod, mod],
        out_specs=[rows(d), rows(d)],
        out_shape=[jax.ShapeDtypeStruct((m, d), _F32), jax.ShapeDtypeStruct((m, d), _BF16)],
        compiler_params=_cparams(("parallel",), OUTPROJ_VMEM),
        name="outproj",
    )(fox, rw, x2, w1, w2, gt, g2, sh, sc)


def _mlp_kernel(h_ref, wu_ref, wd_ref, x1_ref, gt_ref, o_ref):
    @pl.when(pl.program_id(1) == 0)
    def _():
        o_ref[...] = x1_ref[...]

    u = jnp.dot(h_ref[...], wu_ref[...], preferred_element_type=_F32)
    act = jnp.square(jnp.maximum(u, 0.0)).astype(_BF16)
    o_ref[...] += gt_ref[0] * jnp.dot(act, wd_ref[...], preferred_element_type=_F32)


def _mlp(h2, wu, wd, x1, gt, seq):
    m, d = x1.shape
    ff = wu.shape[1]
    tm, tf = MLP_ROWS, MLP_HIDDEN
    per_b = seq // tm
    rows = lambda **kw: pl.BlockSpec((tm, d), lambda i, f: (i, 0), **kw)
    return pl.pallas_call(
        _mlp_kernel,
        grid=(m // tm, ff // tf),
        in_specs=[rows(),
                  pl.BlockSpec((d, tf), lambda i, f: (0, f)),
                  pl.BlockSpec((tf, d), lambda i, f: (f, 0)),
                  rows(),
                  pl.BlockSpec((1, 1, d), lambda i, f: (i // per_b, 0, 0))],
        out_specs=rows(),
        out_shape=jax.ShapeDtypeStruct((m, d), _F32),
        compiler_params=_cparams(("parallel", "arbitrary"), MLP_VMEM),
        name="mlp",
    )(h2, wu, wd, x1, gt)


def _pad_rows(a, rows):
    return jnp.pad(a, ((0, rows - a.shape[0]), (0, 0)))


def _pad_lanes(a, width):
    return jnp.pad(a, ((0, 0), (0, width - a.shape[1])))


def _pack_kernel(wt_ref, o_ref):
    o_f = 3 * D_HEADS
    o_r = o_f + N_HEADS
    o_w = o_r + 3 * D_HEADS
    o_a = o_w + DECAY_LORA
    o_g = o_a + ICLR_LORA
    lanes = wt_ref.shape[1]
    dst = 0
    for src, size in ((0, o_f), (o_r, o_w - o_r), (o_w, DECAY_LORA), (o_f, N_HEADS),
                      (None, PAIR - DECAY_LORA - N_HEADS), (o_a, ICLR_LORA), (None, PAIR - ICLR_LORA),
                      (o_g, GATE_LORA)):
        if src is None:
            o_ref[dst:dst + size, :] = jnp.zeros((size, lanes), _BF16)
        else:
            o_ref[dst:dst + size, :] = wt_ref[src:src + size, :].astype(_BF16)
        dst += size


def _pack_w_in(w):
    wt = w.T
    n, d = wt.shape
    tl = PACK_LANES
    return pl.pallas_call(
        _pack_kernel,
        grid=(d // tl,),
        in_specs=[pl.BlockSpec((n, tl), lambda i: (0, i))],
        out_specs=pl.BlockSpec((6 * D_HEADS + SMALL_W, tl), lambda i: (0, i)),
        out_shape=jax.ShapeDtypeStruct((6 * D_HEADS + SMALL_W, d), _BF16),
        compiler_params=_cparams(("parallel",), PACK_VMEM),
        name="pack_w_in",
    )(wt)


def _pack_shift_mu(mu):
    o_w = 3 * D_HEADS
    o_a = o_w + DECAY_LORA
    o_g = o_a + ICLR_LORA
    return jnp.concatenate([jnp.zeros((1, 3 * D_HEADS), _F32), mu[:, :o_w], _pad_lanes(mu[:, o_w:o_a], PAIR),
                            _pad_lanes(mu[:, o_a:o_g], PAIR), mu[:, o_g:]], axis=1)


def kernel(x, c, w_ada, b_ada, norm1_g, w_in, b_forget, q_norm_g, k_norm_g, fox_out_g, shift_mu, w0, decay_b, a0, a_b, g_b, k_k, k_a, r_k, lnx_g, lnx_b, w_out, norm2_g, w_mlp_up, w_mlp_down):
    batch, seq, d = x.shape
    depth = w_ada.shape[0]
    x2 = x.reshape(batch * seq, d)
    row = lambda a: a.reshape(1, -1)
    for l in range(depth):
        mod = _ada(c, w_ada[l], b_ada[l])
        sh1, sc1, gt1, sh2, sc2, gt2 = [m.reshape(batch, 1, d) for m in jnp.split(mod, 6, axis=-1)]

        pfox, prw = _proj(x2, row(norm1_g[l]), sh1, sc1, _pack_w_in(w_in[l]), _pack_shift_mu(row(shift_mu[l])), seq)

        bf_slot = jnp.pad(row(b_forget[l]), ((0, 0), (F_LANE0, PAIR - F_LANE0 - N_HEADS)))
        qt, ka, vt, lora_in = _fox_prep(pfox, prw, jnp.tile(q_norm_g[l], 2).reshape(PAIR, 1),
                                        jnp.tile(k_norm_g[l], 2).reshape(PAIR, 1), bf_slot, batch, seq)
        fox = _fox_attn(qt, ka, vt, row(fox_out_g[l]), batch, seq)

        prm = {
            "w0": row(w0[l]), "a0": row(a0[l]), "k_k": row(k_k[l]), "k_a": row(k_a[l]),
            "r_k": row(r_k[l]), "lnx_g": row(lnx_g[l]), "lnx_b": row(lnx_b[l]),
            "decay_b": _pad_rows(decay_b[l], PAIR).astype(_BF16),
            "a_b": _pad_rows(a_b[l], PAIR).astype(_BF16),
            "g_b": g_b[l].astype(_BF16),
        }
        rw = _rwkv(prw, lora_in, prm, batch, seq)

        w_o = w_out[l].astype(_BF16)
        x1, h2 = _outproj(fox.reshape(batch * seq, D_HEADS), rw.reshape(batch * seq, D_HEADS), x2,
                          w_o[:D_HEADS], w_o[D_HEADS:], gt1, row(norm2_g[l]), sh2, sc2, seq)
        x2 = _mlp(h2, w_mlp_up[l].astype(_BF16), w_mlp_down[l].astype(_BF16), x1, gt2, seq)
    return x2.reshape(batch, seq, d)
```

```python
import functools

import jax
import jax.numpy as jnp
from jax import lax
from jax.experimental import pallas as pl
from jax.experimental.pallas import tpu as pltpu

_F32 = jnp.float32
_BF16 = jnp.bfloat16

HEAD_DIM = 64
PAIR = 2 * HEAD_DIM
N_HEADS = 16
D_HEADS = N_HEADS * HEAD_DIM
N_PAIRS = N_HEADS // 2
DECAY_LORA = 96
ICLR_LORA = 96
GATE_LORA = 256
SMALL_W = 512
F_LANE0 = DECAY_LORA
NORM_EPS = 1e-6
LNX_EPS = 64e-5
NEG_INF = -1e30
LOG2E = 1.4426950408889634
CHUNK_LOG2 = 6
FOX_ROWS = HEAD_DIM + 16
PAIRS_PER_STEP = 4
CHUNK = 1 << CHUNK_LOG2
MIB = 1024 * 1024

ADA_COLS, ADA_VMEM = 1024, 40
PACK_LANES, PACK_VMEM = 256, 40
PROJ_ROWS, PROJ_COLS, PROJ_VMEM = 1024, 512, 48
FOX_PREP_ROWS, FOX_PREP_VMEM = 512, 40
FOX_KEYS, FOX_VMEM = 512, 48
RWKV_ROWS, RWKV_VMEM = 512, 48
OUTPROJ_ROWS, OUTPROJ_VMEM = 512, 48
MLP_ROWS, MLP_HIDDEN, MLP_VMEM = 512, 2048, 58


def _cparams(sem, vmem_mib):
    return pltpu.CompilerParams(dimension_semantics=sem, vmem_limit_bytes=vmem_mib * MIB)


def _dot(a, b):
    return jnp.dot(a.astype(_BF16), b.astype(_BF16), preferred_element_type=_F32)


def _dot_nt(a, b):
    return lax.dot_general(a.astype(_BF16), b.astype(_BF16), (((1,), (1,)), ((), ())),
                           preferred_element_type=_F32)


def _dot_tn(a, b):
    return lax.dot_general(a.astype(_BF16), b.astype(_BF16), (((0,), (0,)), ((), ())),
                           preferred_element_type=_F32)


def _split3(x):
    h = x.astype(_BF16).astype(_F32)
    r = x - h
    m = r.astype(_BF16).astype(_F32)
    l = (r - m).astype(_BF16).astype(_F32)
    return h, m, l


def _dot_exact_rhs(a_bf16, x):
    return sum(jnp.dot(a_bf16, p.astype(_BF16), preferred_element_type=_F32) for p in _split3(x))


def _sigmoid(x):
    return 1.0 / (1.0 + jnp.exp(-x))


def _softplus(x):
    return jnp.maximum(x, 0.0) + jnp.log(1.0 + jnp.exp(-jnp.abs(x)))


def _half_sum(x, lo_half):
    s_lo = jnp.sum(jnp.where(lo_half, x, 0.0), axis=-1, keepdims=True)
    s_hi = jnp.sum(jnp.where(lo_half, 0.0, x), axis=-1, keepdims=True)
    return jnp.where(lo_half, s_lo, s_hi)


def _lo_half_mask():
    return lax.broadcasted_iota(jnp.int32, (1, PAIR), 1) < HEAD_DIM


def _ada_kernel(c_ref, w_ref, b_ref, o_ref):
    c = c_ref[...]
    cond = c * _sigmoid(c)
    o_ref[...] = _dot(cond, w_ref[...]) + b_ref[...]


def _ada(c, w_ada, b_ada):
    b, d = c.shape
    n = w_ada.shape[1]
    rows = 8 * pl.cdiv(b, 8)
    tn = ADA_COLS
    cp = jnp.pad(c, ((0, rows - b), (0, 0)))
    out = pl.pallas_call(
        _ada_kernel,
        grid=(n // tn,),
        in_specs=[pl.BlockSpec((rows, d), lambda j: (0, 0)),
                  pl.BlockSpec((d, tn), lambda j: (0, j)),
                  pl.BlockSpec((1, tn), lambda j: (0, j))],
        out_specs=pl.BlockSpec((rows, tn), lambda j: (0, j)),
        out_shape=jax.ShapeDtypeStruct((rows, n), _F32),
        compiler_params=_cparams(("parallel",), ADA_VMEM),
        name="ada",
    )(cp, w_ada, b_ada.reshape(1, n))
    return out[:b]


NORM_ROWS = 128


def _adaln(x, g_ref, sh_ref, sc_ref):
    y = x * lax.rsqrt(jnp.mean(x * x, axis=-1, keepdims=True) + NORM_EPS)
    return (y * (g_ref[...] * (1.0 + sc_ref[0])) + sh_ref[0]).astype(_BF16)


def _proj_kernel(x_ref, g_ref, sh_ref, sc_ref, w_ref, mu_ref, fox_ref, rw_ref, h_ref, carry_ref,
                 *, tm, per_b):
    i = pl.program_id(0)
    j = pl.program_id(1)

    @pl.when(j == 0)
    def _():
        for c in range(tm // NORM_ROWS):
            rows = slice(c * NORM_ROWS, (c + 1) * NORM_ROWS)
            h_ref[rows, :] = _adaln(x_ref[rows, :], g_ref, sh_ref, sc_ref)

    y = lax.dot_general(h_ref[...], w_ref[...], (((1,), (1,)), ((), ())), preferred_element_type=_F32)
    first_row = lax.broadcasted_iota(jnp.int32, (tm, 1), 0) == 0
    above = jnp.where(lax.rem(i, per_b) == 0, 0.0, carry_ref[j, 0:1, :])
    prev = jnp.where(first_row, above, pltpu.roll(y, 1, axis=0))
    carry_ref[j, 0:1, :] = y[tm - 1:tm, :]
    fox_ref[...] = y.astype(_BF16)
    rw_ref[...] = y + (prev - y) * mu_ref[...]


def _proj(x2, g, sh, sc, wt, mu, seq):
    m, d = x2.shape
    n = wt.shape[0]
    tm, tn = PROJ_ROWS, PROJ_COLS
    per_b = seq // tm
    fox_tiles = 3 * D_HEADS // tn
    rw_tiles = n // tn - fox_tiles
    return pl.pallas_call(
        functools.partial(_proj_kernel, tm=tm, per_b=per_b),
        grid=(m // tm, n // tn),
        in_specs=[pl.BlockSpec((tm, d), lambda i, j: (i, 0)),
                  pl.BlockSpec((1, d), lambda i, j: (0, 0)),
                  pl.BlockSpec((1, 1, d), lambda i, j: (i // per_b, 0, 0)),
                  pl.BlockSpec((1, 1, d), lambda i, j: (i // per_b, 0, 0)),
                  pl.BlockSpec((tn, d), lambda i, j: (j, 0)),
                  pl.BlockSpec((1, tn), lambda i, j: (0, j))],
        out_specs=[pl.BlockSpec((tm, tn), lambda i, j: (i, jnp.minimum(j, fox_tiles))),
                   pl.BlockSpec((tm, tn), lambda i, j: (i, jnp.where(j < fox_tiles, rw_tiles, j - fox_tiles)))],
        out_shape=[jax.ShapeDtypeStruct((m, (fox_tiles + 1) * tn), _BF16),
                   jax.ShapeDtypeStruct((m, (rw_tiles + 1) * tn), _F32)],
        scratch_shapes=[pltpu.VMEM((tm, d), _BF16), pltpu.VMEM((n // tn, 8, tn), _F32)],
        compiler_params=_cparams(("arbitrary", "arbitrary"), PROJ_VMEM),
        name="proj",
    )(x2, g, sh, sc, wt, mu)


def _fox_prep_kernel(q_ref, k_ref, v_ref, s_ref, qg_ref, kg_ref, bf_ref, qt_ref, ka_ref, vt_ref,
                     lo_ref, carry_ref, *, tt):
    @pl.when(pl.program_id(1) == 0)
    def _():
        carry_ref[...] = jnp.zeros_like(carry_ref)

    lo_ref[:, 0:PAIR] = jnp.tanh(s_ref[:, 0:PAIR]).astype(_BF16)
    lo_ref[:, PAIR:2 * PAIR] = s_ref[:, PAIR:2 * PAIR].astype(_BF16)
    lo_ref[:, 2 * PAIR:] = _sigmoid(s_ref[:, 2 * PAIR:]).astype(_BF16)

    z = s_ref[:, 0:PAIR] + bf_ref[...]
    logf = -_softplus(-z)
    row = lax.broadcasted_iota(jnp.int32, (tt, tt), 0)
    col = lax.broadcasted_iota(jnp.int32, (tt, tt), 1)
    tri = jnp.where(row >= col, 1.0, 0.0).astype(_BF16)
    dcum = carry_ref[0:1, :] + _dot_exact_rhs(tri, logf)
    carry_ref[0:1, :] = dcum[tt - 1:tt, :]
    d1, d2, d3 = _split3((dcum * LOG2E).T[F_LANE0:F_LANE0 + N_HEADS, :])

    r8 = lax.broadcasted_iota(jnp.int32, (8, tt), 0)
    lo_rows = lax.broadcasted_iota(jnp.int32, (PAIR, 1), 0) < HEAD_DIM
    ones_row = jnp.where(r8 == 0, 1.0, 0.0)
    fill = jnp.zeros((FOX_ROWS - HEAD_DIM - 8, tt), _F32)
    wide = jnp.zeros((PAIR - FOX_ROWS, tt), _F32)
    gq = jnp.broadcast_to(qg_ref[...] * (HEAD_DIM ** -0.5 * LOG2E), (PAIR, tt))
    gk = jnp.broadcast_to(kg_ref[...], (PAIR, tt))

    def normed(x_t, gain):
        sq = x_t * x_t
        ms = jnp.where(lo_rows, jnp.sum(sq[:HEAD_DIM], axis=0, keepdims=True),
                       jnp.sum(sq[HEAD_DIM:], axis=0, keepdims=True)) * (1.0 / HEAD_DIM)
        return x_t * lax.rsqrt(ms + NORM_EPS) * gain

    for p in range(N_PAIRS):
        cols = slice(p * PAIR, (p + 1) * PAIR)
        qn = normed(q_ref[:, cols].astype(_F32).T, gq)
        kn = normed(k_ref[:, cols].astype(_F32).T, gk)
        vt = v_ref[:, cols].astype(_F32).T
        for hh in range(2):
            h = 2 * p + hh
            a1, a2, a3 = d1[h:h + 1], d2[h:h + 1], d3[h:h + 1]
            aq = jnp.where(r8 < 3, 1.0, jnp.where(r8 == 3, a1, jnp.where(r8 == 4, a2, jnp.where(r8 == 5, a3, 0.0))))
            ak = jnp.where(r8 == 0, -a1, jnp.where(r8 == 1, -a2, jnp.where(r8 == 2, -a3, jnp.where(r8 < 6, 1.0, 0.0))))

            def with_rows(x_t, extra, hh=hh):
                if hh == 0:
                    return jnp.concatenate([x_t[:HEAD_DIM], extra, fill], axis=0)
                return jnp.concatenate([extra, fill, x_t[HEAD_DIM:]], axis=0)

            qt_ref[0, h] = with_rows(qn, aq).astype(_BF16)
            ka_ref[0, h] = jnp.concatenate([with_rows(kn, ak), wide], axis=0).T.astype(_BF16)
            vt_ref[0, h] = with_rows(vt, ones_row).astype(_BF16)


def _fox_prep(pfox, prw, qg2, kg2, bf_slot, batch, seq):
    tt = FOX_PREP_ROWS
    nt = seq // tt
    row_map = lambda c: (lambda b, t: (b * nt + t, c))
    row_sd = jax.ShapeDtypeStruct((batch, N_HEADS, seq, PAIR), _BF16)
    col_sd = jax.ShapeDtypeStruct((batch, N_HEADS, FOX_ROWS, seq), _BF16)
    row_spec = pl.BlockSpec((1, N_HEADS, tt, PAIR), lambda b, t: (b, 0, t, 0))
    col_spec = pl.BlockSpec((1, N_HEADS, FOX_ROWS, tt), lambda b, t: (b, 0, 0, t))
    vec = pl.BlockSpec((1, PAIR), lambda b, t: (0, 0))
    colvec = pl.BlockSpec((PAIR, 1), lambda b, t: (0, 0))
    return pl.pallas_call(
        functools.partial(_fox_prep_kernel, tt=tt),
        grid=(batch, nt),
        in_specs=[pl.BlockSpec((tt, D_HEADS), row_map(0)),
                  pl.BlockSpec((tt, D_HEADS), row_map(1)),
                  pl.BlockSpec((tt, D_HEADS), row_map(2)),
                  pl.BlockSpec((tt, SMALL_W), row_map(3 * D_HEADS // SMALL_W)),
                  colvec, colvec, vec],
        out_specs=[col_spec, row_spec, col_spec, pl.BlockSpec((tt, SMALL_W), row_map(0))],
        out_shape=[col_sd, row_sd, col_sd, jax.ShapeDtypeStruct((batch * seq, SMALL_W), _BF16)],
        scratch_shapes=[pltpu.VMEM((8, PAIR), _F32)],
        compiler_params=_cparams(("parallel", "arbitrary"), FOX_PREP_VMEM),
        name="fox_prep",
    )(pfox, pfox, pfox, prw, qg2, kg2, bf_slot)


def _fox_attn_kernel(qt_ref, k_ref, vt_ref, g_ref, wu_ref, wd_ref, wo_ref, o_ref, wu16_ref, wd16_ref, wo16_ref,
                     m_ref, acc_ref, s_ref, *, tq, tk):
    wu16_ref[...] = wu_ref[...].astype(_BF16)
    wd16_ref[...] = wd_ref[...].astype(_BF16)
    wo16_ref[...] = wo_ref[...].astype(_BF16)
    qi = pl.program_id(2)
    key = lax.broadcasted_iota(jnp.int32, (tk, tq), 0)
    qry = lax.broadcasted_iota(jnp.int32, (tk, tq), 1)
    m_ref[...] = jnp.full(m_ref.shape, NEG_INF, _F32)
    acc_ref[...] = jnp.zeros(acc_ref.shape, _F32)
    pad = jnp.zeros((PAIR - FOX_ROWS, tq), _BF16)
    qt = [jnp.concatenate([qt_ref[0, hh], pad], axis=0) for hh in range(2)]

    def scores(blk, slot, hh, lanes=slice(None)):
        k = k_ref[0, hh, pl.ds(pl.multiple_of(blk * tk, tk), tk), :]
        s_ref[slot, hh, :, lanes] = jnp.dot(k, qt[hh][:, lanes], preferred_element_type=_F32)

    def absorb(blk, slot, hh, mask=None, lanes=slice(None)):
        vt = vt_ref[0, hh, :, pl.ds(pl.multiple_of(blk * tk, tk), tk)]
        st = s_ref[slot, hh, :, lanes]
        if mask is not None:
            st = jnp.where(mask, st, NEG_INF)
        m_old = m_ref[hh, :, lanes]
        m_new = jnp.maximum(m_old, jnp.max(st, axis=0, keepdims=True))
        p = jnp.exp2(st - m_new)
        acc_ref[hh, :, lanes] = jnp.exp2(m_old - m_new) * acc_ref[hh, :, lanes] + jnp.dot(
            vt, p.astype(_BF16), preferred_element_type=_F32)
        m_ref[hh, :, lanes] = m_new

    for hh in range(2):
        scores(0, 0, hh)

    def two_blocks(j2, carry):
        blk = 2 * j2
        for hh in range(2):
            scores(blk + 1, 1, hh)
            absorb(blk, 0, hh)
        for hh in range(2):
            scores(blk + 2, 0, hh)
            absorb(blk + 1, 1, hh)
        return carry

    lax.fori_loop(0, qi, two_blocks, 0)
    upper = slice(tk, tq)
    for hh in range(2):
        scores(2 * qi + 1, 1, hh, upper)
        absorb(2 * qi, 0, hh, key <= qry)
    for hh in range(2):
        absorb(2 * qi + 1, 1, hh, (key <= qry)[:, :tk], upper)

    acc0, acc1 = acc_ref[0], acc_ref[1]
    o0 = acc0[:HEAD_DIM] / acc0[HEAD_DIM:HEAD_DIM + 1]
    o1 = acc1[FOX_ROWS - HEAD_DIM:] / acc1[0:1]
    n0 = o0 * lax.rsqrt(jnp.mean(o0 * o0, axis=0, keepdims=True) + NORM_EPS)
    n1 = o1 * lax.rsqrt(jnp.mean(o1 * o1, axis=0, keepdims=True) + NORM_EPS)
    o_ref[0] = (jnp.concatenate([n0, n1], axis=0).T * g_ref[...]).astype(_BF16)


def _fox_attn(qt, ka, vt, og, w_up, w_down, w_out, batch, seq):
    tk = FOX_KEYS
    tq = 2 * tk
    nq = seq // tq
    steps = batch * N_PAIRS * nq
    step = lambda b, p, i: ((b * N_PAIRS + p) * nq + i, 0)
    slab = lambda w: pl.BlockSpec((w.shape[0] // steps, w.shape[1]), step)
    return pl.pallas_call(
        functools.partial(_fox_attn_kernel, tq=tq, tk=tk),
        grid=(batch, N_PAIRS, nq),
        in_specs=[pl.BlockSpec((1, 2, FOX_ROWS, tq), lambda b, p, i: (b, p, 0, i)),
                  pl.BlockSpec((1, 2, seq, PAIR), lambda b, p, i: (b, p, 0, 0)),
                  pl.BlockSpec((1, 2, FOX_ROWS, seq), lambda b, p, i: (b, p, 0, 0)),
                  pl.BlockSpec((1, PAIR), lambda b, p, i: (0, p)),
                  slab(w_up), slab(w_down), slab(w_out)],
        out_specs=[pl.BlockSpec((1, tq, PAIR), lambda b, p, i: (b, i, p)), slab(w_up), slab(w_down), slab(w_out)],
        out_shape=[jax.ShapeDtypeStruct((batch, seq, D_HEADS), _BF16),
                   jax.ShapeDtypeStruct(w_up.shape, _BF16), jax.ShapeDtypeStruct(w_down.shape, _BF16),
                   jax.ShapeDtypeStruct(w_out.shape, _BF16)],
        scratch_shapes=[pltpu.VMEM((2, 1, tq), _F32), pltpu.VMEM((2, FOX_ROWS, tq), _F32),
                        pltpu.VMEM((2, 2, tk, tq), _F32)],
        compiler_params=_cparams(("parallel", "parallel", "arbitrary"), FOX_VMEM),
        name="fox_attn",
    )(qt, ka, vt, og, w_up, w_down, w_out)


def _stack_heads(x, lo_half):
    return jnp.concatenate([jnp.where(lo_half, x, 0.0), jnp.where(lo_half, 0.0, x)], axis=0)


def _interleave(*gens):
    live = list(gens)
    while live:
        for gen in list(live):
            try:
                next(gen)
            except StopIteration:
                live.remove(gen)


def _rwkv_kernel(r_ref, k_ref, v_ref, lo_ref, w0_ref, a0_ref, kk_ref, ka_ref, rk_ref, lg_ref, lb_ref,
                 db_ref, ab_ref, gb_ref, o_ref, z_ref, lhs_ref, add_ref, dec_ref, bv_ref, gate_ref,
                 *, tt, nt):
    nc = tt // CHUNK
    c2 = 2 * CHUNK
    t = pl.program_id(2)
    wr = lax.rem(t, 2)
    rd = 1 - wr
    lo_half = _lo_half_mask()

    @pl.when(t == 0)
    def _():
        z_ref[...] = jnp.zeros_like(z_ref)
        lhs_ref[:, 1] = jnp.zeros((PAIRS_PER_STEP,) + lhs_ref.shape[2:], lhs_ref.dtype)
        add_ref[:, 1] = jnp.zeros((PAIRS_PER_STEP,) + add_ref.shape[2:], add_ref.dtype)
        dec_ref[:, 1] = jnp.ones((PAIRS_PER_STEP,) + dec_ref.shape[2:], dec_ref.dtype)
        bv_ref[:, 1] = jnp.zeros((PAIRS_PER_STEP,) + bv_ref.shape[2:], bv_ref.dtype)
        gate_ref[:, 1] = jnp.zeros((PAIRS_PER_STEP,) + gate_ref.shape[2:], gate_ref.dtype)

    def advance(q):
        lanes = slice(q * PAIR, (q + 1) * PAIR)
        z = z_ref[q]
        dec = dec_ref[q, rd]
        ys = []
        for c in range(nc):
            res = jnp.dot(lhs_ref[q, rd, c], z.astype(_BF16), preferred_element_type=_F32)
            add = add_ref[q, rd, c]
            ysh = res[:c2] + add[:c2]
            ys.append(ysh[:CHUNK] + ysh[CHUNK:])
            z = dec[:, c:c + 1] * z + res[c2:] + add[c2:]
            yield
        z_ref[q] = z
        y = jnp.concatenate(ys, axis=0)
        mu = _half_sum(y, lo_half) * (1.0 / HEAD_DIM)
        dlt = y - mu
        var = _half_sum(dlt * dlt, lo_half) * (1.0 / HEAD_DIM)
        yn = dlt * lax.rsqrt(var + LNX_EPS) * lg_ref[:, lanes] + lb_ref[:, lanes]
        o_ref[0, :, lanes] = ((yn + bv_ref[q, rd]) * gate_ref[q, rd]).astype(_BF16)
        yield

    def lora():
        w = -_softplus(-(w0_ref[...] + jnp.dot(lo_ref[:, 0:PAIR], db_ref[...],
                                                 preferred_element_type=_F32))) - 0.5
        logd = -jnp.exp(w)
        a = _sigmoid(a0_ref[...] + jnp.dot(lo_ref[:, PAIR:2 * PAIR], ab_ref[...],
                                           preferred_element_type=_F32))
        gate = jnp.dot(lo_ref[:, 2 * PAIR:], gb_ref[...], preferred_element_type=_F32)
        return logd, a, gate

    def prepare(q, logd_all, a_all, gate_all):
        lanes = slice(q * PAIR, (q + 1) * PAIR)
        r = r_ref[:, lanes]
        k = k_ref[:, lanes]
        v = v_ref[:, lanes]
        logd = logd_all[:, lanes]
        a = a_all[:, lanes]
        gate_ref[q, wr] = gate_all[:, lanes]
        kk = k * kk_ref[:, lanes]
        kk = kk * lax.rsqrt(jnp.maximum(_half_sum(kk * kk, lo_half), 1e-24))
        k = k * (1.0 + (a - 1.0) * ka_ref[:, lanes])
        bb = kk * a
        bv_ref[q, wr] = _half_sum(r * k * rk_ref[:, lanes], lo_half) * v
        yield

        pos = lax.broadcasted_iota(jnp.int32, (tt, 1), 0) & (CHUNK - 1)
        lcum = logd
        for j in range(CHUNK_LOG2):
            lcum = lcum + jnp.where(pos >= (1 << j), pltpu.roll(lcum, 1 << j, axis=0), 0.0)
        yield

        r2 = lax.broadcasted_iota(jnp.int32, (c2, c2), 0)
        q2 = lax.broadcasted_iota(jnp.int32, (c2, c2), 1)
        same = (r2 >> CHUNK_LOG2) == (q2 >> CHUNK_LOG2)
        strict = same & (r2 > q2)
        incl = same & (r2 >= q2)
        eye = jnp.where(r2 == q2, 1.0, 0.0)
        zeros_cc = jnp.zeros((c2, PAIR), _F32)

        rng = range(nc)
        sl = [slice(c * CHUNK, (c + 1) * CHUNK) for c in rng]
        lc = [lcum[sl[c]] for c in rng]
        lc_rows = [lc[c][CHUNK - 1:CHUNK, :] for c in rng]
        dec_ref[q, wr] = jnp.exp(jnp.concatenate(lc_rows + [jnp.zeros((PAIR - nc, PAIR), _F32)], axis=0).T)
        e_pos = [jnp.exp(lc[c]) for c in rng]
        e_neg = [jnp.exp(-lc[c]) for c in rng]
        e_end = [jnp.exp(lc_rows[c] - lc[c]) for c in rng]
        rp = [_stack_heads(r[sl[c]] * e_pos[c], lo_half) for c in rng]
        ap = [_stack_heads(-kk[sl[c]] * jnp.exp(lc[c] - logd[sl[c]]), lo_half) for c in rng]
        kq = [_stack_heads(k[sl[c]] * e_neg[c], lo_half) for c in rng]
        bq = [_stack_heads(bb[sl[c]] * e_neg[c], lo_half) for c in rng]
        kc = [_stack_heads(k[sl[c]] * e_end[c], lo_half) for c in rng]
        bc = [_stack_heads(bb[sl[c]] * e_end[c], lo_half) for c in rng]
        vs = [_stack_heads(v[sl[c]], lo_half) for c in rng]
        yield

        s = [_dot_nt(jnp.concatenate([ap[c], rp[c]], axis=0), jnp.concatenate([bq[c], kq[c]], axis=0))
             for c in rng]
        a_ab = [jnp.where(strict, s[c][:c2, :c2], 0.0) for c in rng]
        a_ak = [jnp.where(strict, s[c][:c2, c2:], 0.0) for c in rng]
        a_rb = [jnp.where(incl, s[c][c2:, :c2], 0.0) for c in rng]
        a_rk = [jnp.where(incl, s[c][c2:, c2:], 0.0) for c in rng]
        yield
        akv = [_dot(a_ak[c], vs[c]) for c in rng]

        pw = [_dot(a_ab[c], a_ab[c]) for c in rng]
        tinv = [eye + a_ab[c] for c in rng]
        yield
        for _ in range(CHUNK_LOG2 - 2):
            pt = [_dot(pw[c], jnp.concatenate([pw[c], tinv[c]], axis=1)) for c in rng]
            pw = [pt[c][:, :c2] for c in rng]
            tinv = [tinv[c] + pt[c][:, c2:] for c in rng]
            yield
        q5 = [_dot(tinv[c], jnp.concatenate([ap[c], akv[c]], axis=1)) for c in rng]
        yield
        wu = [q5[c] + _dot(pw[c], q5[c]) for c in rng]
        yield
        x = [jnp.concatenate([wu[c], jnp.concatenate([zeros_cc, vs[c]], axis=1)], axis=0) for c in rng]
        ry = [_dot(jnp.concatenate([a_rb[c], a_rk[c]], axis=1), x[c]) for c in rng]
        mz = [_dot_tn(jnp.concatenate([bc[c], kc[c]], axis=0), x[c]) for c in rng]
        yield
        for c in rng:
            lhs_ref[q, wr, c] = jnp.concatenate([rp[c] + ry[c][:, :PAIR], mz[c][:, :PAIR]], axis=0).astype(_BF16)
            add_ref[q, wr, c] = jnp.concatenate([ry[c][:, PAIR:], mz[c][:, PAIR:]], axis=0)
        yield

    pairs = range(PAIRS_PER_STEP)

    @pl.when(t < nt)
    def _():
        shared = lora()
        _interleave(*[prepare(q, *shared) for q in pairs], *[advance(q) for q in pairs])

    @pl.when(t == nt)
    def _():
        _interleave(*[advance(q) for q in pairs])


def _rwkv(prw, lora_in, prm, batch, seq):
    tt = RWKV_ROWS
    nt = seq // tt
    nc = tt // CHUNK
    npp = PAIRS_PER_STEP
    width = npp * PAIR
    cur = lambda b, t: b * nt + jnp.minimum(t, nt - 1)
    blk = lambda c0: pl.BlockSpec((tt, width), lambda b, p, t, c0=c0: (cur(b, t), c0 + p))
    vec = pl.BlockSpec((1, width), lambda b, p, t: (0, p))
    lora = lambda rows: pl.BlockSpec((rows, width), lambda b, p, t: (0, p))
    per = D_HEADS // width
    return pl.pallas_call(
        functools.partial(_rwkv_kernel, tt=tt, nt=nt),
        grid=(batch, N_PAIRS // npp, nt + 1),
        in_specs=[blk(0), blk(per), blk(2 * per),
                  pl.BlockSpec((tt, SMALL_W), lambda b, p, t: (cur(b, t), 0)),
                  vec, vec, vec, vec, vec, vec, vec,
                  lora(PAIR), lora(PAIR), lora(GATE_LORA)],
        out_specs=pl.BlockSpec((1, tt, width), lambda b, p, t: (b, jnp.maximum(t - 1, 0), p)),
        out_shape=jax.ShapeDtypeStruct((batch, seq, D_HEADS), _BF16),
        scratch_shapes=[pltpu.VMEM((npp, PAIR, PAIR), _F32),
                        pltpu.VMEM((npp, 2, nc, 2 * PAIR, PAIR), _BF16),
                        pltpu.VMEM((npp, 2, nc, 2 * PAIR, PAIR), _F32),
                        pltpu.VMEM((npp, 2, PAIR, PAIR), _F32),
                        pltpu.VMEM((npp, 2, tt, PAIR), _F32),
                        pltpu.VMEM((npp, 2, tt, PAIR), _F32)],
        compiler_params=_cparams(("parallel", "parallel", "arbitrary"), RWKV_VMEM),
        name="rwkv",
    )(prw, prw, prw, lora_in, prm["w0"], prm["a0"], prm["k_k"], prm["k_a"], prm["r_k"],
      prm["lnx_g"], prm["lnx_b"], prm["decay_b"], prm["a_b"], prm["g_b"])


def _outproj_kernel(fox_ref, rw_ref, x_ref, w1_ref, w2_ref, gt_ref, g2_ref, sh_ref, sc_ref,
                    x1_ref, h2_ref):
    y = (jnp.dot(fox_ref[...], w1_ref[...], preferred_element_type=_F32)
         + jnp.dot(rw_ref[...], w2_ref[...], preferred_element_type=_F32))
    x1 = x_ref[...] + gt_ref[0] * y
    x1_ref[...] = x1
    h2_ref[...] = _adaln(x1, g2_ref, sh_ref, sc_ref)


def _outproj(fox, rw, x2, w_o, gt, g2, sh, sc, seq):
    m, d = x2.shape
    tm = OUTPROJ_ROWS
    per_b = seq // tm
    rows = lambda w: pl.BlockSpec((tm, w), lambda i: (i, 0))
    full = lambda a: pl.BlockSpec(a.shape, lambda i: (0, 0))
    mod = pl.BlockSpec((1, 1, d), lambda i: (i // per_b, 0, 0))
    return pl.pallas_call(
        _outproj_kernel,
        grid=(m // tm,),
        in_specs=[rows(D_HEADS), rows(D_HEADS), rows(d),
                  pl.BlockSpec((D_HEADS, d), lambda i: (0, 0)), pl.BlockSpec((D_HEADS, d), lambda i: (1, 0)),
                  mod, full(g2), mod, mod],
        out_specs=[rows(d), rows(d)],
        out_shape=[jax.ShapeDtypeStruct((m, d), _F32), jax.ShapeDtypeStruct((m, d), _BF16)],
        compiler_params=_cparams(("parallel",), OUTPROJ_VMEM),
        name="outproj",
    )(fox, rw, x2, w_o, w_o, gt, g2, sh, sc)


def _mlp_kernel(h_ref, wu_ref, wd_ref, x1_ref, gt_ref, o_ref):
    @pl.when(pl.program_id(1) == 0)
    def _():
        o_ref[...] = x1_ref[...]

    u = jnp.dot(h_ref[...], wu_ref[...], preferred_element_type=_F32)
    act = jnp.square(jnp.maximum(u, 0.0)).astype(_BF16)
    o_ref[...] += gt_ref[0] * jnp.dot(act, wd_ref[...], preferred_element_type=_F32)


def _mlp(h2, wu, wd, x1, gt, seq):
    m, d = x1.shape
    ff = wu.shape[1]
    tm, tf = MLP_ROWS, MLP_HIDDEN
    per_b = seq // tm
    rows = lambda **kw: pl.BlockSpec((tm, d), lambda i, f: (i, 0), **kw)
    return pl.pallas_call(
        _mlp_kernel,
        grid=(m // tm, ff // tf),
        in_specs=[rows(),
                  pl.BlockSpec((d, tf), lambda i, f: (0, f)),
                  pl.BlockSpec((tf, d), lambda i, f: (f, 0)),
                  rows(),
                  pl.BlockSpec((1, 1, d), lambda i, f: (i // per_b, 0, 0))],
        out_specs=rows(),
        out_shape=jax.ShapeDtypeStruct((m, d), _F32),
        compiler_params=_cparams(("parallel", "arbitrary"), MLP_VMEM),
        name="mlp",
    )(h2, wu, wd, x1, gt)


def _pad_rows(a, rows):
    return jnp.pad(a, ((0, rows - a.shape[0]), (0, 0)))


def _pad_lanes(a, width):
    return jnp.pad(a, ((0, 0), (0, width - a.shape[1])))


def _pack_kernel(wt_ref, o_ref):
    o_f = 3 * D_HEADS
    o_r = o_f + N_HEADS
    o_w = o_r + 3 * D_HEADS
    o_a = o_w + DECAY_LORA
    o_g = o_a + ICLR_LORA
    lanes = wt_ref.shape[1]
    dst = 0
    for src, size in ((0, o_f), (o_r, o_w - o_r), (o_w, DECAY_LORA), (o_f, N_HEADS),
                      (None, PAIR - DECAY_LORA - N_HEADS), (o_a, ICLR_LORA), (None, PAIR - ICLR_LORA),
                      (o_g, GATE_LORA)):
        if src is None:
            o_ref[dst:dst + size, :] = jnp.zeros((size, lanes), _BF16)
        else:
            o_ref[dst:dst + size, :] = wt_ref[src:src + size, :].astype(_BF16)
        dst += size


def _pack_w_in(w):
    wt = w.T
    n, d = wt.shape
    tl = PACK_LANES
    return pl.pallas_call(
        _pack_kernel,
        grid=(d // tl,),
        in_specs=[pl.BlockSpec((n, tl), lambda i: (0, i))],
        out_specs=pl.BlockSpec((6 * D_HEADS + SMALL_W, tl), lambda i: (0, i)),
        out_shape=jax.ShapeDtypeStruct((6 * D_HEADS + SMALL_W, d), _BF16),
        compiler_params=_cparams(("parallel",), PACK_VMEM),
        name="pack_w_in",
    )(wt)


def _pack_shift_mu(mu):
    o_w = 3 * D_HEADS
    o_a = o_w + DECAY_LORA
    o_g = o_a + ICLR_LORA
    return jnp.concatenate([jnp.zeros((1, 3 * D_HEADS), _F32), mu[:, :o_w], _pad_lanes(mu[:, o_w:o_a], PAIR),
                            _pad_lanes(mu[:, o_a:o_g], PAIR), mu[:, o_g:]], axis=1)


def kernel(x, c, w_ada, b_ada, norm1_g, w_in, b_forget, q_norm_g, k_norm_g, fox_out_g, shift_mu, w0, decay_b, a0, a_b, g_b, k_k, k_a, r_k, lnx_g, lnx_b, w_out, norm2_g, w_mlp_up, w_mlp_down):
    batch, seq, d = x.shape
    depth = w_ada.shape[0]
    x2 = x.reshape(batch * seq, d)
    row = lambda a: a.reshape(1, -1)
    for l in range(depth):
        mod = _ada(c, w_ada[l], b_ada[l])
        sh1, sc1, gt1, sh2, sc2, gt2 = [m.reshape(batch, 1, d) for m in jnp.split(mod, 6, axis=-1)]

        pfox, prw = _proj(x2, row(norm1_g[l]), sh1, sc1, _pack_w_in(w_in[l]), _pack_shift_mu(row(shift_mu[l])), seq)

        bf_slot = jnp.pad(row(b_forget[l]), ((0, 0), (F_LANE0, PAIR - F_LANE0 - N_HEADS)))
        qt, ka, vt, lora_in = _fox_prep(pfox, prw, jnp.tile(q_norm_g[l], 2).reshape(PAIR, 1),
                                        jnp.tile(k_norm_g[l], 2).reshape(PAIR, 1), bf_slot, batch, seq)
        fox, w_up16, w_down16, w_out16 = _fox_attn(qt, ka, vt, row(fox_out_g[l]), w_mlp_up[l], w_mlp_down[l],
                                                   w_out[l], batch, seq)

        prm = {
            "w0": row(w0[l]), "a0": row(a0[l]), "k_k": row(k_k[l]), "k_a": row(k_a[l]),
            "r_k": row(r_k[l]), "lnx_g": row(lnx_g[l]), "lnx_b": row(lnx_b[l]),
            "decay_b": _pad_rows(decay_b[l], PAIR).astype(_BF16),
            "a_b": _pad_rows(a_b[l], PAIR).astype(_BF16),
            "g_b": g_b[l].astype(_BF16),
        }
        rw = _rwkv(prw, lora_in, prm, batch, seq)

        x1, h2 = _outproj(fox.reshape(batch * seq, D_HEADS), rw.reshape(batch * seq, D_HEADS), x2,
                          w_out16, gt1, row(norm2_g[l]), sh2, sc2, seq)
        x2 = _mlp(h2, w_up16, w_down16, x1, gt2, seq)
    return x2.reshape(batch, seq, d)
```

```python
import functools

import jax
import jax.numpy as jnp
from jax import lax
from jax.experimental import pallas as pl
from jax.experimental.pallas import tpu as pltpu

_F32 = jnp.float32
_BF16 = jnp.bfloat16

HEAD_DIM = 64
PAIR = 2 * HEAD_DIM
N_HEADS = 16
D_HEADS = N_HEADS * HEAD_DIM
N_PAIRS = N_HEADS // 2
DECAY_LORA = 96
ICLR_LORA = 96
GATE_LORA = 256
SMALL_W = 512
F_LANE0 = DECAY_LORA
NORM_EPS = 1e-6
LNX_EPS = 64e-5
NEG_INF = -1e30
LOG2E = 1.4426950408889634
CHUNK_LOG2 = 6
FOX_ROWS = HEAD_DIM + 16
PAIRS_PER_STEP = 4
CHUNK = 1 << CHUNK_LOG2
MIB = 1024 * 1024

ADA_COLS, ADA_VMEM = 1024, 40
PACK_LANES, PACK_VMEM = 256, 40
PROJ_ROWS, PROJ_COLS, PROJ_VMEM = 1024, 512, 48
FOX_PREP_ROWS, FOX_PREP_VMEM = 512, 40
FOX_KEYS, FOX_VMEM = 512, 48
RWKV_ROWS, RWKV_VMEM = 512, 48
OUTPROJ_ROWS, OUTPROJ_VMEM = 512, 48
MLP_ROWS, MLP_HIDDEN, MLP_VMEM = 512, 2048, 58


def _cparams(sem, vmem_mib):
    return pltpu.CompilerParams(dimension_semantics=sem, vmem_limit_bytes=vmem_mib * MIB)


def _dot(a, b):
    return jnp.dot(a.astype(_BF16), b.astype(_BF16), preferred_element_type=_F32)


def _dot_nt(a, b):
    return lax.dot_general(a.astype(_BF16), b.astype(_BF16), (((1,), (1,)), ((), ())),
                           preferred_element_type=_F32)


def _dot_tn(a, b):
    return lax.dot_general(a.astype(_BF16), b.astype(_BF16), (((0,), (0,)), ((), ())),
                           preferred_element_type=_F32)


def _split3(x):
    h = x.astype(_BF16).astype(_F32)
    r = x - h
    m = r.astype(_BF16).astype(_F32)
    l = (r - m).astype(_BF16).astype(_F32)
    return h, m, l


def _dot_exact_rhs(a_bf16, x):
    return sum(jnp.dot(a_bf16, p.astype(_BF16), preferred_element_type=_F32) for p in _split3(x))


def _sigmoid(x):
    return 1.0 / (1.0 + jnp.exp(-x))


def _softplus(x):
    return jnp.maximum(x, 0.0) + jnp.log(1.0 + jnp.exp(-jnp.abs(x)))


def _half_sum(x, lo_half):
    s_lo = jnp.sum(jnp.where(lo_half, x, 0.0), axis=-1, keepdims=True)
    s_hi = jnp.sum(jnp.where(lo_half, 0.0, x), axis=-1, keepdims=True)
    return jnp.where(lo_half, s_lo, s_hi)


def _lo_half_mask():
    return lax.broadcasted_iota(jnp.int32, (1, PAIR), 1) < HEAD_DIM


def _ada_kernel(c_ref, w_ref, b_ref, o_ref):
    c = c_ref[...]
    cond = c * _sigmoid(c)
    o_ref[...] = _dot(cond, w_ref[...]) + b_ref[...]


def _ada(c, w_ada, b_ada):
    b, d = c.shape
    n = w_ada.shape[1]
    rows = 8 * pl.cdiv(b, 8)
    tn = ADA_COLS
    cp = jnp.pad(c, ((0, rows - b), (0, 0)))
    out = pl.pallas_call(
        _ada_kernel,
        grid=(n // tn,),
        in_specs=[pl.BlockSpec((rows, d), lambda j: (0, 0)),
                  pl.BlockSpec((d, tn), lambda j: (0, j)),
                  pl.BlockSpec((1, tn), lambda j: (0, j))],
        out_specs=pl.BlockSpec((rows, tn), lambda j: (0, j)),
        out_shape=jax.ShapeDtypeStruct((rows, n), _F32),
        compiler_params=_cparams(("parallel",), ADA_VMEM),
        name="ada",
    )(cp, w_ada, b_ada.reshape(1, n))
    return out[:b]


NORM_ROWS = 128


def _adaln(x, g_ref, sh_ref, sc_ref):
    y = x * lax.rsqrt(jnp.mean(x * x, axis=-1, keepdims=True) + NORM_EPS)
    return (y * (g_ref[...] * (1.0 + sc_ref[0])) + sh_ref[0]).astype(_BF16)


def _proj_kernel(x_ref, g_ref, sh_ref, sc_ref, w_ref, mu_ref, fox_ref, rw_ref, h_ref, carry_ref,
                 *, tm, per_b):
    i = pl.program_id(0)
    j = pl.program_id(1)

    @pl.when(j == 0)
    def _():
        for c in range(tm // NORM_ROWS):
            rows = slice(c * NORM_ROWS, (c + 1) * NORM_ROWS)
            h_ref[rows, :] = _adaln(x_ref[rows, :], g_ref, sh_ref, sc_ref)

    y = lax.dot_general(h_ref[...], w_ref[...], (((1,), (1,)), ((), ())), preferred_element_type=_F32)
    first_row = lax.broadcasted_iota(jnp.int32, (tm, 1), 0) == 0
    above = jnp.where(lax.rem(i, per_b) == 0, 0.0, carry_ref[j, 0:1, :])
    prev = jnp.where(first_row, above, pltpu.roll(y, 1, axis=0))
    carry_ref[j, 0:1, :] = y[tm - 1:tm, :]
    fox_ref[...] = y.astype(_BF16)
    rw_ref[...] = y + (prev - y) * mu_ref[...]


def _proj(x2, g, sh, sc, wt, mu, seq):
    m, d = x2.shape
    n = wt.shape[0]
    tm, tn = PROJ_ROWS, PROJ_COLS
    per_b = seq // tm
    fox_tiles = 3 * D_HEADS // tn
    rw_tiles = n // tn - fox_tiles
    return pl.pallas_call(
        functools.partial(_proj_kernel, tm=tm, per_b=per_b),
        grid=(m // tm, n // tn),
        in_specs=[pl.BlockSpec((tm, d), lambda i, j: (i, 0)),
                  pl.BlockSpec((1, d), lambda i, j: (0, 0)),
                  pl.BlockSpec((1, 1, d), lambda i, j: (i // per_b, 0, 0)),
                  pl.BlockSpec((1, 1, d), lambda i, j: (i // per_b, 0, 0)),
                  pl.BlockSpec((tn, d), lambda i, j: (j, 0)),
                  pl.BlockSpec((1, tn), lambda i, j: (0, j))],
        out_specs=[pl.BlockSpec((tm, tn), lambda i, j: (i, jnp.minimum(j, fox_tiles))),
                   pl.BlockSpec((tm, tn), lambda i, j: (i, jnp.where(j < fox_tiles, rw_tiles, j - fox_tiles)))],
        out_shape=[jax.ShapeDtypeStruct((m, (fox_tiles + 1) * tn), _BF16),
                   jax.ShapeDtypeStruct((m, (rw_tiles + 1) * tn), _F32)],
        scratch_shapes=[pltpu.VMEM((tm, d), _BF16), pltpu.VMEM((n // tn, 8, tn), _F32)],
        compiler_params=_cparams(("arbitrary", "arbitrary"), PROJ_VMEM),
        name="proj",
    )(x2, g, sh, sc, wt, mu)


def _fox_prep_kernel(q_ref, k_ref, v_ref, s_ref, qg_ref, kg_ref, bf_ref, qt_ref, ka_ref, vt_ref,
                     lo_ref, carry_ref, *, tt):
    @pl.when(pl.program_id(1) == 0)
    def _():
        carry_ref[...] = jnp.zeros_like(carry_ref)

    lo_ref[:, 0:PAIR] = jnp.tanh(s_ref[:, 0:PAIR]).astype(_BF16)
    lo_ref[:, PAIR:2 * PAIR] = s_ref[:, PAIR:2 * PAIR].astype(_BF16)
    lo_ref[:, 2 * PAIR:] = _sigmoid(s_ref[:, 2 * PAIR:]).astype(_BF16)

    z = s_ref[:, 0:PAIR] + bf_ref[...]
    logf = -_softplus(-z)
    row = lax.broadcasted_iota(jnp.int32, (tt, tt), 0)
    col = lax.broadcasted_iota(jnp.int32, (tt, tt), 1)
    tri = jnp.where(row >= col, 1.0, 0.0).astype(_BF16)
    dcum = carry_ref[0:1, :] + _dot_exact_rhs(tri, logf)
    carry_ref[0:1, :] = dcum[tt - 1:tt, :]
    d1, d2, d3 = _split3((dcum * LOG2E).T[F_LANE0:F_LANE0 + N_HEADS, :])

    r8 = lax.broadcasted_iota(jnp.int32, (8, tt), 0)
    lo_rows = lax.broadcasted_iota(jnp.int32, (PAIR, 1), 0) < HEAD_DIM
    ones_row = jnp.where(r8 == 0, 1.0, 0.0)
    fill = jnp.zeros((FOX_ROWS - HEAD_DIM - 8, tt), _F32)
    wide = jnp.zeros((PAIR - FOX_ROWS, tt), _F32)
    gq = jnp.broadcast_to(qg_ref[...] * (HEAD_DIM ** -0.5 * LOG2E), (PAIR, tt))
    gk = jnp.broadcast_to(kg_ref[...], (PAIR, tt))

    def normed(x_t, gain):
        sq = x_t * x_t
        ms = jnp.where(lo_rows, jnp.sum(sq[:HEAD_DIM], axis=0, keepdims=True),
                       jnp.sum(sq[HEAD_DIM:], axis=0, keepdims=True)) * (1.0 / HEAD_DIM)
        return x_t * lax.rsqrt(ms + NORM_EPS) * gain

    for p in range(N_PAIRS):
        cols = slice(p * PAIR, (p + 1) * PAIR)
        qn = normed(q_ref[:, cols].astype(_F32).T, gq)
        kn = normed(k_ref[:, cols].astype(_F32).T, gk)
        vt = v_ref[:, cols].astype(_F32).T
        for hh in range(2):
            h = 2 * p + hh
            a1, a2, a3 = d1[h:h + 1], d2[h:h + 1], d3[h:h + 1]
            aq = jnp.where(r8 < 3, 1.0, jnp.where(r8 == 3, a1, jnp.where(r8 == 4, a2, jnp.where(r8 == 5, a3, 0.0))))
            ak = jnp.where(r8 == 0, -a1, jnp.where(r8 == 1, -a2, jnp.where(r8 == 2, -a3, jnp.where(r8 < 6, 1.0, 0.0))))

            def with_rows(x_t, extra, hh=hh):
                if hh == 0:
                    return jnp.concatenate([x_t[:HEAD_DIM], extra, fill], axis=0)
                return jnp.concatenate([extra, fill, x_t[HEAD_DIM:]], axis=0)

            qt_ref[0, h] = with_rows(qn, aq).astype(_BF16)
            ka_ref[0, h] = jnp.concatenate([with_rows(kn, ak), wide], axis=0).T.astype(_BF16)
            vt_ref[0, h] = with_rows(vt, ones_row).astype(_BF16)


def _fox_prep(pfox, prw, qg2, kg2, bf_slot, batch, seq):
    tt = FOX_PREP_ROWS
    nt = seq // tt
    row_map = lambda c: (lambda b, t: (b * nt + t, c))
    row_sd = jax.ShapeDtypeStruct((batch, N_HEADS, seq, PAIR), _BF16)
    col_sd = jax.ShapeDtypeStruct((batch, N_HEADS, FOX_ROWS, seq), _BF16)
    row_spec = pl.BlockSpec((1, N_HEADS, tt, PAIR), lambda b, t: (b, 0, t, 0))
    col_spec = pl.BlockSpec((1, N_HEADS, FOX_ROWS, tt), lambda b, t: (b, 0, 0, t))
    vec = pl.BlockSpec((1, PAIR), lambda b, t: (0, 0))
    colvec = pl.BlockSpec((PAIR, 1), lambda b, t: (0, 0))
    return pl.pallas_call(
        functools.partial(_fox_prep_kernel, tt=tt),
        grid=(batch, nt),
        in_specs=[pl.BlockSpec((tt, D_HEADS), row_map(0)),
                  pl.BlockSpec((tt, D_HEADS), row_map(1)),
                  pl.BlockSpec((tt, D_HEADS), row_map(2)),
                  pl.BlockSpec((tt, SMALL_W), row_map(3 * D_HEADS // SMALL_W)),
                  colvec, colvec, vec],
        out_specs=[col_spec, row_spec, col_spec, pl.BlockSpec((tt, SMALL_W), row_map(0))],
        out_shape=[col_sd, row_sd, col_sd, jax.ShapeDtypeStruct((batch * seq, SMALL_W), _BF16)],
        scratch_shapes=[pltpu.VMEM((8, PAIR), _F32)],
        compiler_params=_cparams(("parallel", "arbitrary"), FOX_PREP_VMEM),
        name="fox_prep",
    )(pfox, pfox, pfox, prw, qg2, kg2, bf_slot)


def _fox_attn_kernel(qt_ref, k_ref, vt_ref, g_ref, wu_ref, wd_ref, wo_ref, o_ref, wu16_ref, wd16_ref, wo16_ref,
                     m_ref, acc_ref, s_ref, *, tq, tk):
    wu16_ref[...] = wu_ref[...].astype(_BF16)
    wd16_ref[...] = wd_ref[...].astype(_BF16)
    wo16_ref[...] = wo_ref[...].astype(_BF16)
    qi = pl.program_id(2)
    key = lax.broadcasted_iota(jnp.int32, (tk, tq), 0)
    qry = lax.broadcasted_iota(jnp.int32, (tk, tq), 1)
    m_ref[...] = jnp.full(m_ref.shape, NEG_INF, _F32)
    acc_ref[...] = jnp.zeros(acc_ref.shape, _F32)
    pad = jnp.zeros((PAIR - FOX_ROWS, tq), _BF16)
    qt = [jnp.concatenate([qt_ref[0, hh], pad], axis=0) for hh in range(2)]

    def scores(blk, slot, hh, lanes=slice(None)):
        k = k_ref[0, hh, pl.ds(pl.multiple_of(blk * tk, tk), tk), :]
        s_ref[slot, hh, :, lanes] = jnp.dot(k, qt[hh][:, lanes], preferred_element_type=_F32)

    def absorb(blk, slot, hh, mask=None, lanes=slice(None)):
        vt = vt_ref[0, hh, :, pl.ds(pl.multiple_of(blk * tk, tk), tk)]
        st = s_ref[slot, hh, :, lanes]
        if mask is not None:
            st = jnp.where(mask, st, NEG_INF)
        m_old = m_ref[hh, :, lanes]
        m_new = jnp.maximum(m_old, jnp.max(st, axis=0, keepdims=True))
        p = jnp.exp2(st - m_new)
        acc_ref[hh, :, lanes] = jnp.exp2(m_old - m_new) * acc_ref[hh, :, lanes] + jnp.dot(
            vt, p.astype(_BF16), preferred_element_type=_F32)
        m_ref[hh, :, lanes] = m_new

    for hh in range(2):
        scores(0, 0, hh)

    def two_blocks(j2, carry):
        blk = 2 * j2
        for hh in range(2):
            scores(blk + 1, 1, hh)
            absorb(blk, 0, hh)
        for hh in range(2):
            scores(blk + 2, 0, hh)
            absorb(blk + 1, 1, hh)
        return carry

    lax.fori_loop(0, qi, two_blocks, 0)
    upper = slice(tk, tq)
    for hh in range(2):
        scores(2 * qi + 1, 1, hh, upper)
        absorb(2 * qi, 0, hh, key <= qry)
    for hh in range(2):
        absorb(2 * qi + 1, 1, hh, (key <= qry)[:, :tk], upper)

    acc0, acc1 = acc_ref[0], acc_ref[1]
    o0 = acc0[:HEAD_DIM] / acc0[HEAD_DIM:HEAD_DIM + 1]
    o1 = acc1[FOX_ROWS - HEAD_DIM:] / acc1[0:1]
    n0 = o0 * lax.rsqrt(jnp.mean(o0 * o0, axis=0, keepdims=True) + NORM_EPS)
    n1 = o1 * lax.rsqrt(jnp.mean(o1 * o1, axis=0, keepdims=True) + NORM_EPS)
    o_ref[0] = (jnp.concatenate([n0, n1], axis=0).T * g_ref[...]).astype(_BF16)


def _fox_attn(qt, ka, vt, og, w_up, w_down, w_out, batch, seq):
    tk = FOX_KEYS
    tq = 2 * tk
    nq = seq // tq
    steps = batch * N_PAIRS * nq
    bf16_rows = 16
    assert seq % tq == 0 and all(w.shape[0] % (steps * bf16_rows) == 0 for w in (w_up, w_down, w_out))
    step = lambda b, p, i: ((b * N_PAIRS + p) * nq + i, 0)
    slab = lambda w: pl.BlockSpec((w.shape[0] // steps, w.shape[1]), step)
    return pl.pallas_call(
        functools.partial(_fox_attn_kernel, tq=tq, tk=tk),
        grid=(batch, N_PAIRS, nq),
        in_specs=[pl.BlockSpec((1, 2, FOX_ROWS, tq), lambda b, p, i: (b, p, 0, i)),
                  pl.BlockSpec((1, 2, seq, PAIR), lambda b, p, i: (b, p, 0, 0)),
                  pl.BlockSpec((1, 2, FOX_ROWS, seq), lambda b, p, i: (b, p, 0, 0)),
                  pl.BlockSpec((1, PAIR), lambda b, p, i: (0, p)),
                  slab(w_up), slab(w_down), slab(w_out)],
        out_specs=[pl.BlockSpec((1, tq, PAIR), lambda b, p, i: (b, i, p)), slab(w_up), slab(w_down), slab(w_out)],
        out_shape=[jax.ShapeDtypeStruct((batch, seq, D_HEADS), _BF16),
                   jax.ShapeDtypeStruct(w_up.shape, _BF16), jax.ShapeDtypeStruct(w_down.shape, _BF16),
                   jax.ShapeDtypeStruct(w_out.shape, _BF16)],
        scratch_shapes=[pltpu.VMEM((2, 1, tq), _F32), pltpu.VMEM((2, FOX_ROWS, tq), _F32),
                        pltpu.VMEM((2, 2, tk, tq), _F32)],
        compiler_params=_cparams(("parallel", "parallel", "arbitrary"), FOX_VMEM),
        name="fox_attn",
    )(qt, ka, vt, og, w_up, w_down, w_out)


def _stack_heads(x, lo_half):
    return jnp.concatenate([jnp.where(lo_half, x, 0.0), jnp.where(lo_half, 0.0, x)], axis=0)


def _interleave(*gens):
    live = list(gens)
    while live:
        for gen in list(live):
            try:
                next(gen)
            except StopIteration:
                live.remove(gen)


def _rwkv_kernel(r_ref, k_ref, v_ref, lo_ref, w0_ref, a0_ref, kk_ref, ka_ref, rk_ref, lg_ref, lb_ref,
                 db_ref, ab_ref, gb_ref, o_ref, z_ref, lhs_ref, add_ref, dec_ref, bv_ref, gate_ref,
                 *, tt, nt):
    nc = tt // CHUNK
    c2 = 2 * CHUNK
    t = pl.program_id(2)
    wr = lax.rem(t, 2)
    rd = 1 - wr
    lo_half = _lo_half_mask()

    @pl.when(t == 0)
    def _():
        z_ref[...] = jnp.zeros_like(z_ref)
        lhs_ref[:, 1] = jnp.zeros((PAIRS_PER_STEP,) + lhs_ref.shape[2:], lhs_ref.dtype)
        add_ref[:, 1] = jnp.zeros((PAIRS_PER_STEP,) + add_ref.shape[2:], add_ref.dtype)
        dec_ref[:, 1] = jnp.ones((PAIRS_PER_STEP,) + dec_ref.shape[2:], dec_ref.dtype)
        bv_ref[:, 1] = jnp.zeros((PAIRS_PER_STEP,) + bv_ref.shape[2:], bv_ref.dtype)
        gate_ref[:, 1] = jnp.zeros((PAIRS_PER_STEP,) + gate_ref.shape[2:], gate_ref.dtype)

    def advance(q):
        lanes = slice(q * PAIR, (q + 1) * PAIR)
        z = z_ref[q]
        dec = dec_ref[q, rd]
        ys = []
        for c in range(nc):
            res = jnp.dot(lhs_ref[q, rd, c], z.astype(_BF16), preferred_element_type=_F32)
            add = add_ref[q, rd, c]
            ysh = res[:c2] + add[:c2]
            ys.append(ysh[:CHUNK] + ysh[CHUNK:])
            z = dec[:, c:c + 1] * z + res[c2:] + add[c2:]
            yield
        z_ref[q] = z
        y = jnp.concatenate(ys, axis=0)
        mu = _half_sum(y, lo_half) * (1.0 / HEAD_DIM)
        dlt = y - mu
        var = _half_sum(dlt * dlt, lo_half) * (1.0 / HEAD_DIM)
        yn = dlt * lax.rsqrt(var + LNX_EPS) * lg_ref[:, lanes] + lb_ref[:, lanes]
        o_ref[0, :, lanes] = ((yn + bv_ref[q, rd]) * gate_ref[q, rd]).astype(_BF16)
        yield

    def lora():
        w = -_softplus(-(w0_ref[...] + jnp.dot(lo_ref[:, 0:PAIR], db_ref[...],
                                                 preferred_element_type=_F32))) - 0.5
        logd = -jnp.exp(w)
        a = _sigmoid(a0_ref[...] + jnp.dot(lo_ref[:, PAIR:2 * PAIR], ab_ref[...],
                                           preferred_element_type=_F32))
        gate = jnp.dot(lo_ref[:, 2 * PAIR:], gb_ref[...], preferred_element_type=_F32)
        return logd, a, gate

    def prepare(q, logd_all, a_all, gate_all):
        lanes = slice(q * PAIR, (q + 1) * PAIR)
        r = r_ref[:, lanes]
        k = k_ref[:, lanes]
        v = v_ref[:, lanes]
        logd = logd_all[:, lanes]
        a = a_all[:, lanes]
        gate_ref[q, wr] = gate_all[:, lanes]
        kk = k * kk_ref[:, lanes]
        kk = kk * lax.rsqrt(jnp.maximum(_half_sum(kk * kk, lo_half), 1e-24))
        k = k * (1.0 + (a - 1.0) * ka_ref[:, lanes])
        bb = kk * a
        bv_ref[q, wr] = _half_sum(r * k * rk_ref[:, lanes], lo_half) * v
        yield

        pos = lax.broadcasted_iota(jnp.int32, (tt, 1), 0) & (CHUNK - 1)
        lcum = logd
        for j in range(CHUNK_LOG2):
            lcum = lcum + jnp.where(pos >= (1 << j), pltpu.roll(lcum, 1 << j, axis=0), 0.0)
        yield

        r2 = lax.broadcasted_iota(jnp.int32, (c2, c2), 0)
        q2 = lax.broadcasted_iota(jnp.int32, (c2, c2), 1)
        same = (r2 >> CHUNK_LOG2) == (q2 >> CHUNK_LOG2)
        strict = same & (r2 > q2)
        incl = same & (r2 >= q2)
        eye = jnp.where(r2 == q2, 1.0, 0.0)
        zeros_cc = jnp.zeros((c2, PAIR), _F32)

        rng = range(nc)
        sl = [slice(c * CHUNK, (c + 1) * CHUNK) for c in rng]
        lc = [lcum[sl[c]] for c in rng]
        lc_rows = [lc[c][CHUNK - 1:CHUNK, :] for c in rng]
        dec_ref[q, wr] = jnp.exp(jnp.concatenate(lc_rows + [jnp.zeros((PAIR - nc, PAIR), _F32)], axis=0).T)
        e_pos = [jnp.exp(lc[c]) for c in rng]
        e_neg = [jnp.exp(-lc[c]) for c in rng]
        e_end = [jnp.exp(lc_rows[c] - lc[c]) for c in rng]
        rp = [_stack_heads(r[sl[c]] * e_pos[c], lo_half) for c in rng]
        ap = [_stack_heads(-kk[sl[c]] * jnp.exp(lc[c] - logd[sl[c]]), lo_half) for c in rng]
        kq = [_stack_heads(k[sl[c]] * e_neg[c], lo_half) for c in rng]
        bq = [_stack_heads(bb[sl[c]] * e_neg[c], lo_half) for c in rng]
        kc = [_stack_heads(k[sl[c]] * e_end[c], lo_half) for c in rng]
        bc = [_stack_heads(bb[sl[c]] * e_end[c], lo_half) for c in rng]
        vs = [_stack_heads(v[sl[c]], lo_half) for c in rng]
        yield

        s = [_dot_nt(jnp.concatenate([ap[c], rp[c]], axis=0), jnp.concatenate([bq[c], kq[c]], axis=0))
             for c in rng]
        a_ab = [jnp.where(strict, s[c][:c2, :c2], 0.0) for c in rng]
        a_ak = [jnp.where(strict, s[c][:c2, c2:], 0.0) for c in rng]
        a_rb = [jnp.where(incl, s[c][c2:, :c2], 0.0) for c in rng]
        a_rk = [jnp.where(incl, s[c][c2:, c2:], 0.0) for c in rng]
        yield
        akv = [_dot(a_ak[c], vs[c]) for c in rng]

        pw = [_dot(a_ab[c], a_ab[c]) for c in rng]
        tinv = [eye + a_ab[c] for c in rng]
        yield
        for _ in range(CHUNK_LOG2 - 2):
            pt = [_dot(pw[c], jnp.concatenate([pw[c], tinv[c]], axis=1)) for c in rng]
            pw = [pt[c][:, :c2] for c in rng]
            tinv = [tinv[c] + pt[c][:, c2:] for c in rng]
            yield
        q5 = [_dot(tinv[c], jnp.concatenate([ap[c], akv[c]], axis=1)) for c in rng]
        yield
        wu = [q5[c] + _dot(pw[c], q5[c]) for c in rng]
        yield
        x = [jnp.concatenate([wu[c], jnp.concatenate([zeros_cc, vs[c]], axis=1)], axis=0) for c in rng]
        ry = [_dot(jnp.concatenate([a_rb[c], a_rk[c]], axis=1), x[c]) for c in rng]
        mz = [_dot_tn(jnp.concatenate([bc[c], kc[c]], axis=0), x[c]) for c in rng]
        yield
        for c in rng:
            lhs_ref[q, wr, c] = jnp.concatenate([rp[c] + ry[c][:, :PAIR], mz[c][:, :PAIR]], axis=0).astype(_BF16)
            add_ref[q, wr, c] = jnp.concatenate([ry[c][:, PAIR:], mz[c][:, PAIR:]], axis=0)
        yield

    pairs = range(PAIRS_PER_STEP)

    @pl.when(t < nt)
    def _():
        shared = lora()
        _interleave(*[prepare(q, *shared) for q in pairs], *[advance(q) for q in pairs])

    @pl.when(t == nt)
    def _():
        _interleave(*[advance(q) for q in pairs])


def _rwkv(prw, lora_in, prm, batch, seq):
    tt = RWKV_ROWS
    nt = seq // tt
    nc = tt // CHUNK
    npp = PAIRS_PER_STEP
    width = npp * PAIR
    cur = lambda b, t: b * nt + jnp.minimum(t, nt - 1)
    blk = lambda c0: pl.BlockSpec((tt, width), lambda b, p, t, c0=c0: (cur(b, t), c0 + p))
    vec = pl.BlockSpec((1, width), lambda b, p, t: (0, p))
    lora = lambda rows: pl.BlockSpec((rows, width), lambda b, p, t: (0, p))
    per = D_HEADS // width
    return pl.pallas_call(
        functools.partial(_rwkv_kernel, tt=tt, nt=nt),
        grid=(batch, N_PAIRS // npp, nt + 1),
        in_specs=[blk(0), blk(per), blk(2 * per),
                  pl.BlockSpec((tt, SMALL_W), lambda b, p, t: (cur(b, t), 0)),
                  vec, vec, vec, vec, vec, vec, vec,
                  lora(PAIR), lora(PAIR), lora(GATE_LORA)],
        out_specs=pl.BlockSpec((1, tt, width), lambda b, p, t: (b, jnp.maximum(t - 1, 0), p)),
        out_shape=jax.ShapeDtypeStruct((batch, seq, D_HEADS), _BF16),
        scratch_shapes=[pltpu.VMEM((npp, PAIR, PAIR), _F32),
                        pltpu.VMEM((npp, 2, nc, 2 * PAIR, PAIR), _BF16),
                        pltpu.VMEM((npp, 2, nc, 2 * PAIR, PAIR), _F32),
                        pltpu.VMEM((npp, 2, PAIR, PAIR), _F32),
                        pltpu.VMEM((npp, 2, tt, PAIR), _F32),
                        pltpu.VMEM((npp, 2, tt, PAIR), _F32)],
        compiler_params=_cparams(("parallel", "parallel", "arbitrary"), RWKV_VMEM),
        name="rwkv",
    )(prw, prw, prw, lora_in, prm["w0"], prm["a0"], prm["k_k"], prm["k_a"], prm["r_k"],
      prm["lnx_g"], prm["lnx_b"], prm["decay_b"], prm["a_b"], prm["g_b"])


def _outproj_kernel(fox_ref, rw_ref, x_ref, w1_ref, w2_ref, gt_ref, g2_ref, sh_ref, sc_ref,
                    x1_ref, h2_ref):
    y = (jnp.dot(fox_ref[...], w1_ref[...], preferred_element_type=_F32)
         + jnp.dot(rw_ref[...], w2_ref[...], preferred_element_type=_F32))
    x1 = x_ref[...] + gt_ref[0] * y
    x1_ref[...] = x1
    h2_ref[...] = _adaln(x1, g2_ref, sh_ref, sc_ref)


def _outproj(fox, rw, x2, w_o, gt, g2, sh, sc, seq):
    m, d = x2.shape
    tm = OUTPROJ_ROWS
    per_b = seq // tm
    rows = lambda w: pl.BlockSpec((tm, w), lambda i: (i, 0))
    full = lambda a: pl.BlockSpec(a.shape, lambda i: (0, 0))
    mod = pl.BlockSpec((1, 1, d), lambda i: (i // per_b, 0, 0))
    return pl.pallas_call(
        _outproj_kernel,
        grid=(m // tm,),
        in_specs=[rows(D_HEADS), rows(D_HEADS), rows(d),
                  pl.BlockSpec((D_HEADS, d), lambda i: (0, 0)), pl.BlockSpec((D_HEADS, d), lambda i: (1, 0)),
                  mod, full(g2), mod, mod],
        out_specs=[rows(d), rows(d)],
        out_shape=[jax.ShapeDtypeStruct((m, d), _F32), jax.ShapeDtypeStruct((m, d), _BF16)],
        compiler_params=_cparams(("parallel",), OUTPROJ_VMEM),
        name="outproj",
    )(fox, rw, x2, w_o, w_o, gt, g2, sh, sc)


def _mlp_kernel(h_ref, wu_ref, wd_ref, x1_ref, gt_ref, o_ref):
    @pl.when(pl.program_id(1) == 0)
    def _():
        o_ref[...] = x1_ref[...]

    u = jnp.dot(h_ref[...], wu_ref[...], preferred_element_type=_F32)
    act = jnp.square(jnp.maximum(u, 0.0)).astype(_BF16)
    o_ref[...] += gt_ref[0] * jnp.dot(act, wd_ref[...], preferred_element_type=_F32)


def _mlp(h2, wu, wd, x1, gt, seq):
    m, d = x1.shape
    ff = wu.shape[1]
    tm, tf = MLP_ROWS, MLP_HIDDEN
    per_b = seq // tm
    rows = lambda **kw: pl.BlockSpec((tm, d), lambda i, f: (i, 0), **kw)
    return pl.pallas_call(
        _mlp_kernel,
        grid=(m // tm, ff // tf),
        in_specs=[rows(),
                  pl.BlockSpec((d, tf), lambda i, f: (0, f)),
                  pl.BlockSpec((tf, d), lambda i, f: (f, 0)),
                  rows(),
                  pl.BlockSpec((1, 1, d), lambda i, f: (i // per_b, 0, 0))],
        out_specs=rows(),
        out_shape=jax.ShapeDtypeStruct((m, d), _F32),
        compiler_params=_cparams(("parallel", "arbitrary"), MLP_VMEM),
        name="mlp",
    )(h2, wu, wd, x1, gt)


def _pad_rows(a, rows):
    return jnp.pad(a, ((0, rows - a.shape[0]), (0, 0)))


def _pad_lanes(a, width):
    return jnp.pad(a, ((0, 0), (0, width - a.shape[1])))


def _pack_kernel(wt_ref, o_ref):
    o_f = 3 * D_HEADS
    o_r = o_f + N_HEADS
    o_w = o_r + 3 * D_HEADS
    o_a = o_w + DECAY_LORA
    o_g = o_a + ICLR_LORA
    lanes = wt_ref.shape[1]
    dst = 0
    for src, size in ((0, o_f), (o_r, o_w - o_r), (o_w, DECAY_LORA), (o_f, N_HEADS),
                      (None, PAIR - DECAY_LORA - N_HEADS), (o_a, ICLR_LORA), (None, PAIR - ICLR_LORA),
                      (o_g, GATE_LORA)):
        if src is None:
            o_ref[dst:dst + size, :] = jnp.zeros((size, lanes), _BF16)
        else:
            o_ref[dst:dst + size, :] = wt_ref[src:src + size, :].astype(_BF16)
        dst += size


def _pack_w_in(w):
    wt = w.T
    n, d = wt.shape
    tl = PACK_LANES
    return pl.pallas_call(
        _pack_kernel,
        grid=(d // tl,),
        in_specs=[pl.BlockSpec((n, tl), lambda i: (0, i))],
        out_specs=pl.BlockSpec((6 * D_HEADS + SMALL_W, tl), lambda i: (0, i)),
        out_shape=jax.ShapeDtypeStruct((6 * D_HEADS + SMALL_W, d), _BF16),
        compiler_params=_cparams(("parallel",), PACK_VMEM),
        name="pack_w_in",
    )(wt)


def _pack_shift_mu(mu):
    o_w = 3 * D_HEADS
    o_a = o_w + DECAY_LORA
    o_g = o_a + ICLR_LORA
    return jnp.concatenate([jnp.zeros((1, 3 * D_HEADS), _F32), mu[:, :o_w], _pad_lanes(mu[:, o_w:o_a], PAIR),
                            _pad_lanes(mu[:, o_a:o_g], PAIR), mu[:, o_g:]], axis=1)


def kernel(x, c, w_ada, b_ada, norm1_g, w_in, b_forget, q_norm_g, k_norm_g, fox_out_g, shift_mu, w0, decay_b, a0, a_b, g_b, k_k, k_a, r_k, lnx_g, lnx_b, w_out, norm2_g, w_mlp_up, w_mlp_down):
    batch, seq, d = x.shape
    depth = w_ada.shape[0]
    assert w_in.shape[2] == 6 * D_HEADS + N_HEADS + DECAY_LORA + ICLR_LORA + GATE_LORA
    assert seq % max(PROJ_ROWS, RWKV_ROWS, FOX_PREP_ROWS, OUTPROJ_ROWS, MLP_ROWS) == 0 and d % PACK_LANES == 0
    assert w_mlp_up.shape[2] % MLP_HIDDEN == 0 and (6 * d) % ADA_COLS == 0
    x2 = x.reshape(batch * seq, d)
    row = lambda a: a.reshape(1, -1)
    for l in range(depth):
        mod = _ada(c, w_ada[l], b_ada[l])
        sh1, sc1, gt1, sh2, sc2, gt2 = [m.reshape(batch, 1, d) for m in jnp.split(mod, 6, axis=-1)]

        pfox, prw = _proj(x2, row(norm1_g[l]), sh1, sc1, _pack_w_in(w_in[l]), _pack_shift_mu(row(shift_mu[l])), seq)

        bf_slot = jnp.pad(row(b_forget[l]), ((0, 0), (F_LANE0, PAIR - F_LANE0 - N_HEADS)))
        qt, ka, vt, lora_in = _fox_prep(pfox, prw, jnp.tile(q_norm_g[l], 2).reshape(PAIR, 1),
                                        jnp.tile(k_norm_g[l], 2).reshape(PAIR, 1), bf_slot, batch, seq)
        fox, w_up16, w_down16, w_out16 = _fox_attn(qt, ka, vt, row(fox_out_g[l]), w_mlp_up[l], w_mlp_down[l],
                                                   w_out[l], batch, seq)

        prm = {
            "w0": row(w0[l]), "a0": row(a0[l]), "k_k": row(k_k[l]), "k_a": row(k_a[l]),
            "r_k": row(r_k[l]), "lnx_g": row(lnx_g[l]), "lnx_b": row(lnx_b[l]),
            "decay_b": _pad_rows(decay_b[l], PAIR).astype(_BF16),
            "a_b": _pad_rows(a_b[l], PAIR).astype(_BF16),
            "g_b": g_b[l].astype(_BF16),
        }
        rw = _rwkv(prw, lora_in, prm, batch, seq)

        x1, h2 = _outproj(fox.reshape(batch * seq, D_HEADS), rw.reshape(batch * seq, D_HEADS), x2,
                          w_out16, gt1, row(norm2_g[l]), sh2, sc2, seq)
        x2 = _mlp(h2, w_up16, w_down16, x1, gt2, seq)
    return x2.reshape(batch, seq, d)
```

```python
import functools

import jax
import jax.numpy as jnp
from jax import lax
from jax.experimental import pallas as pl
from jax.experimental.pallas import tpu as pltpu

_F32 = jnp.float32
_BF16 = jnp.bfloat16

HEAD_DIM = 64
PAIR = 2 * HEAD_DIM
N_HEADS = 16
D_HEADS = N_HEADS * HEAD_DIM
N_PAIRS = N_HEADS // 2
DECAY_LORA = 96
ICLR_LORA = 96
GATE_LORA = 256
SMALL_W = 512
F_LANE0 = DECAY_LORA
NORM_EPS = 1e-6
LNX_EPS = 64e-5
NEG_INF = -1e30
LOG2E = 1.4426950408889634
CHUNK_LOG2 = 6
FOX_ROWS = HEAD_DIM + 16
PAIRS_PER_STEP = 4
CHUNK = 1 << CHUNK_LOG2
MIB = 1024 * 1024

ADA_COLS, ADA_VMEM = 1024, 40
PACK_LANES, PACK_VMEM = 256, 40
PROJ_ROWS, PROJ_COLS, PROJ_VMEM = 1024, 512, 48
FOX_PREP_ROWS, FOX_PREP_VMEM = 1024, 52
FOX_KEYS, FOX_VMEM = 512, 48
RWKV_ROWS, RWKV_VMEM = 512, 48
OUTPROJ_ROWS, OUTPROJ_VMEM = 512, 48
MLP_ROWS, MLP_HIDDEN, MLP_VMEM = 512, 2048, 58


def _cparams(sem, vmem_mib):
    return pltpu.CompilerParams(dimension_semantics=sem, vmem_limit_bytes=vmem_mib * MIB)


def _dot(a, b):
    return jnp.dot(a.astype(_BF16), b.astype(_BF16), preferred_element_type=_F32)


def _dot_nt(a, b):
    return lax.dot_general(a.astype(_BF16), b.astype(_BF16), (((1,), (1,)), ((), ())),
                           preferred_element_type=_F32)


def _dot_tn(a, b):
    return lax.dot_general(a.astype(_BF16), b.astype(_BF16), (((0,), (0,)), ((), ())),
                           preferred_element_type=_F32)


def _split3(x):
    h = x.astype(_BF16).astype(_F32)
    r = x - h
    m = r.astype(_BF16).astype(_F32)
    l = (r - m).astype(_BF16).astype(_F32)
    return h, m, l


def _dot_exact_rhs(a_bf16, x):
    return sum(jnp.dot(a_bf16, p.astype(_BF16), preferred_element_type=_F32) for p in _split3(x))


def _sigmoid(x):
    return 1.0 / (1.0 + jnp.exp(-x))


def _softplus(x):
    return jnp.maximum(x, 0.0) + jnp.log(1.0 + jnp.exp(-jnp.abs(x)))


def _half_sum(x, lo_half):
    s_lo = jnp.sum(jnp.where(lo_half, x, 0.0), axis=-1, keepdims=True)
    s_hi = jnp.sum(jnp.where(lo_half, 0.0, x), axis=-1, keepdims=True)
    return jnp.where(lo_half, s_lo, s_hi)


def _lo_half_mask():
    return lax.broadcasted_iota(jnp.int32, (1, PAIR), 1) < HEAD_DIM


def _ada_kernel(c_ref, w_ref, b_ref, o_ref):
    c = c_ref[...]
    cond = c * _sigmoid(c)
    o_ref[...] = _dot(cond, w_ref[...]) + b_ref[...]


def _ada(c, w_ada, b_ada):
    b, d = c.shape
    n = w_ada.shape[1]
    rows = 8 * pl.cdiv(b, 8)
    tn = ADA_COLS
    cp = jnp.pad(c, ((0, rows - b), (0, 0)))
    out = pl.pallas_call(
        _ada_kernel,
        grid=(n // tn,),
        in_specs=[pl.BlockSpec((rows, d), lambda j: (0, 0)),
                  pl.BlockSpec((d, tn), lambda j: (0, j)),
                  pl.BlockSpec((1, tn), lambda j: (0, j))],
        out_specs=pl.BlockSpec((rows, tn), lambda j: (0, j)),
        out_shape=jax.ShapeDtypeStruct((rows, n), _F32),
        compiler_params=_cparams(("parallel",), ADA_VMEM),
        name="ada",
    )(cp, w_ada, b_ada.reshape(1, n))
    return out[:b]


NORM_ROWS = 128


def _adaln(x, g_ref, sh_ref, sc_ref):
    y = x * lax.rsqrt(jnp.mean(x * x, axis=-1, keepdims=True) + NORM_EPS)
    return (y * (g_ref[...] * (1.0 + sc_ref[0])) + sh_ref[0]).astype(_BF16)


def _proj_kernel(x_ref, g_ref, sh_ref, sc_ref, w_ref, mu_ref, fox_ref, rw_ref, h_ref, carry_ref,
                 *, tm, per_b):
    i = pl.program_id(0)
    j = pl.program_id(1)

    @pl.when(j == 0)
    def _():
        for c in range(tm // NORM_ROWS):
            rows = slice(c * NORM_ROWS, (c + 1) * NORM_ROWS)
            h_ref[rows, :] = _adaln(x_ref[rows, :], g_ref, sh_ref, sc_ref)

    y = lax.dot_general(h_ref[...], w_ref[...], (((1,), (1,)), ((), ())), preferred_element_type=_F32)
    first_row = lax.broadcasted_iota(jnp.int32, (tm, 1), 0) == 0
    above = jnp.where(lax.rem(i, per_b) == 0, 0.0, carry_ref[j, 0:1, :])
    prev = jnp.where(first_row, above, pltpu.roll(y, 1, axis=0))
    carry_ref[j, 0:1, :] = y[tm - 1:tm, :]
    fox_ref[...] = y.astype(_BF16)
    rw_ref[...] = y + (prev - y) * mu_ref[...]


def _proj(x2, g, sh, sc, wt, mu, seq):
    m, d = x2.shape
    n = wt.shape[0]
    tm, tn = PROJ_ROWS, PROJ_COLS
    per_b = seq // tm
    fox_tiles = 3 * D_HEADS // tn
    rw_tiles = n // tn - fox_tiles
    return pl.pallas_call(
        functools.partial(_proj_kernel, tm=tm, per_b=per_b),
        grid=(m // tm, n // tn),
        in_specs=[pl.BlockSpec((tm, d), lambda i, j: (i, 0)),
                  pl.BlockSpec((1, d), lambda i, j: (0, 0)),
                  pl.BlockSpec((1, 1, d), lambda i, j: (i // per_b, 0, 0)),
                  pl.BlockSpec((1, 1, d), lambda i, j: (i // per_b, 0, 0)),
                  pl.BlockSpec((tn, d), lambda i, j: (j, 0)),
                  pl.BlockSpec((1, tn), lambda i, j: (0, j))],
        out_specs=[pl.BlockSpec((tm, tn), lambda i, j: (i, jnp.minimum(j, fox_tiles))),
                   pl.BlockSpec((tm, tn), lambda i, j: (i, jnp.where(j < fox_tiles, rw_tiles, j - fox_tiles)))],
        out_shape=[jax.ShapeDtypeStruct((m, (fox_tiles + 1) * tn), _BF16),
                   jax.ShapeDtypeStruct((m, (rw_tiles + 1) * tn), _F32)],
        scratch_shapes=[pltpu.VMEM((tm, d), _BF16), pltpu.VMEM((n // tn, 8, tn), _F32)],
        compiler_params=_cparams(("arbitrary", "arbitrary"), PROJ_VMEM),
        name="proj",
    )(x2, g, sh, sc, wt, mu)


def _fox_prep_kernel(q_ref, k_ref, v_ref, s_ref, qg_ref, kg_ref, bf_ref, qt_ref, ka_ref, vt_ref,
                     lo_ref, carry_ref, *, tt):
    @pl.when(pl.program_id(1) == 0)
    def _():
        carry_ref[...] = jnp.zeros_like(carry_ref)

    lo_ref[:, 0:PAIR] = jnp.tanh(s_ref[:, 0:PAIR]).astype(_BF16)
    lo_ref[:, PAIR:2 * PAIR] = s_ref[:, PAIR:2 * PAIR].astype(_BF16)
    lo_ref[:, 2 * PAIR:] = _sigmoid(s_ref[:, 2 * PAIR:]).astype(_BF16)

    z = s_ref[:, 0:PAIR] + bf_ref[...]
    logf = -_softplus(-z)
    row = lax.broadcasted_iota(jnp.int32, (tt, tt), 0)
    col = lax.broadcasted_iota(jnp.int32, (tt, tt), 1)
    tri = jnp.where(row >= col, 1.0, 0.0).astype(_BF16)
    dcum = carry_ref[0:1, :] + _dot_exact_rhs(tri, logf)
    carry_ref[0:1, :] = dcum[tt - 1:tt, :]
    d1, d2, d3 = _split3((dcum * LOG2E).T[F_LANE0:F_LANE0 + N_HEADS, :])

    r8 = lax.broadcasted_iota(jnp.int32, (8, tt), 0)
    lo_rows = lax.broadcasted_iota(jnp.int32, (PAIR, 1), 0) < HEAD_DIM
    ones_row = jnp.where(r8 == 0, 1.0, 0.0)
    fill = jnp.zeros((FOX_ROWS - HEAD_DIM - 8, tt), _F32)
    wide = jnp.zeros((PAIR - FOX_ROWS, tt), _F32)
    gq = jnp.broadcast_to(qg_ref[...] * (HEAD_DIM ** -0.5 * LOG2E), (PAIR, tt))
    gk = jnp.broadcast_to(kg_ref[...], (PAIR, tt))

    def normed(x_t, gain):
        sq = x_t * x_t
        ms = jnp.where(lo_rows, jnp.sum(sq[:HEAD_DIM], axis=0, keepdims=True),
                       jnp.sum(sq[HEAD_DIM:], axis=0, keepdims=True)) * (1.0 / HEAD_DIM)
        return x_t * lax.rsqrt(ms + NORM_EPS) * gain

    for p in range(N_PAIRS):
        cols = slice(p * PAIR, (p + 1) * PAIR)
        qn = normed(q_ref[:, cols].astype(_F32).T, gq)
        kn = normed(k_ref[:, cols].astype(_F32).T, gk)
        vt = v_ref[:, cols].astype(_F32).T
        for hh in range(2):
            h = 2 * p + hh
            a1, a2, a3 = d1[h:h + 1], d2[h:h + 1], d3[h:h + 1]
            aq = jnp.where(r8 < 3, 1.0, jnp.where(r8 == 3, a1, jnp.where(r8 == 4, a2, jnp.where(r8 == 5, a3, 0.0))))
            ak = jnp.where(r8 == 0, -a1, jnp.where(r8 == 1, -a2, jnp.where(r8 == 2, -a3, jnp.where(r8 < 6, 1.0, 0.0))))

            def with_rows(x_t, extra, hh=hh):
                if hh == 0:
                    return jnp.concatenate([x_t[:HEAD_DIM], extra, fill], axis=0)
                return jnp.concatenate([extra, fill, x_t[HEAD_DIM:]], axis=0)

            qt_ref[0, h] = with_rows(qn, aq).astype(_BF16)
            ka_ref[0, h] = jnp.concatenate([with_rows(kn, ak), wide], axis=0).T.astype(_BF16)
            vt_ref[0, h] = with_rows(vt, ones_row).astype(_BF16)


def _fox_prep(pfox, prw, qg2, kg2, bf_slot, batch, seq):
    tt = FOX_PREP_ROWS
    nt = seq // tt
    row_map = lambda c: (lambda b, t: (b * nt + t, c))
    row_sd = jax.ShapeDtypeStruct((batch, N_HEADS, seq, PAIR), _BF16)
    col_sd = jax.ShapeDtypeStruct((batch, N_HEADS, FOX_ROWS, seq), _BF16)
    row_spec = pl.BlockSpec((1, N_HEADS, tt, PAIR), lambda b, t: (b, 0, t, 0))
    col_spec = pl.BlockSpec((1, N_HEADS, FOX_ROWS, tt), lambda b, t: (b, 0, 0, t))
    vec = pl.BlockSpec((1, PAIR), lambda b, t: (0, 0))
    colvec = pl.BlockSpec((PAIR, 1), lambda b, t: (0, 0))
    return pl.pallas_call(
        functools.partial(_fox_prep_kernel, tt=tt),
        grid=(batch, nt),
        in_specs=[pl.BlockSpec((tt, D_HEADS), row_map(0)),
                  pl.BlockSpec((tt, D_HEADS), row_map(1)),
                  pl.BlockSpec((tt, D_HEADS), row_map(2)),
                  pl.BlockSpec((tt, SMALL_W), row_map(3 * D_HEADS // SMALL_W)),
                  colvec, colvec, vec],
        out_specs=[col_spec, row_spec, col_spec, pl.BlockSpec((tt, SMALL_W), row_map(0))],
        out_shape=[col_sd, row_sd, col_sd, jax.ShapeDtypeStruct((batch * seq, SMALL_W), _BF16)],
        scratch_shapes=[pltpu.VMEM((8, PAIR), _F32)],
        compiler_params=_cparams(("parallel", "arbitrary"), FOX_PREP_VMEM),
        name="fox_prep",
    )(pfox, pfox, pfox, prw, qg2, kg2, bf_slot)


def _fox_attn_kernel(qt_ref, k_ref, vt_ref, g_ref, wu_ref, wd_ref, wo_ref, o_ref, wu16_ref, wd16_ref, wo16_ref,
                     m_ref, acc_ref, s_ref, *, tq, tk):
    wu16_ref[...] = wu_ref[...].astype(_BF16)
    wd16_ref[...] = wd_ref[...].astype(_BF16)
    wo16_ref[...] = wo_ref[...].astype(_BF16)
    qi = pl.program_id(2)
    key = lax.broadcasted_iota(jnp.int32, (tk, tq), 0)
    qry = lax.broadcasted_iota(jnp.int32, (tk, tq), 1)
    m_ref[...] = jnp.full(m_ref.shape, NEG_INF, _F32)
    acc_ref[...] = jnp.zeros(acc_ref.shape, _F32)
    pad = jnp.zeros((PAIR - FOX_ROWS, tq), _BF16)
    qt = [jnp.concatenate([qt_ref[0, hh], pad], axis=0) for hh in range(2)]

    def scores(blk, slot, hh, lanes=slice(None)):
        k = k_ref[0, hh, pl.ds(pl.multiple_of(blk * tk, tk), tk), :]
        s_ref[slot, hh, :, lanes] = jnp.dot(k, qt[hh][:, lanes], preferred_element_type=_F32)

    def absorb(blk, slot, hh, mask=None, lanes=slice(None)):
        vt = vt_ref[0, hh, :, pl.ds(pl.multiple_of(blk * tk, tk), tk)]
        st = s_ref[slot, hh, :, lanes]
        if mask is not None:
            st = jnp.where(mask, st, NEG_INF)
        m_old = m_ref[hh, :, lanes]
        m_new = jnp.maximum(m_old, jnp.max(st, axis=0, keepdims=True))
        p = jnp.exp2(st - m_new)
        acc_ref[hh, :, lanes] = jnp.exp2(m_old - m_new) * acc_ref[hh, :, lanes] + jnp.dot(
            vt, p.astype(_BF16), preferred_element_type=_F32)
        m_ref[hh, :, lanes] = m_new

    for hh in range(2):
        scores(0, 0, hh)

    def two_blocks(j2, carry):
        blk = 2 * j2
        for hh in range(2):
            scores(blk + 1, 1, hh)
            absorb(blk, 0, hh)
        for hh in range(2):
            scores(blk + 2, 0, hh)
            absorb(blk + 1, 1, hh)
        return carry

    lax.fori_loop(0, qi, two_blocks, 0)
    upper = slice(tk, tq)
    for hh in range(2):
        scores(2 * qi + 1, 1, hh, upper)
        absorb(2 * qi, 0, hh, key <= qry)
    for hh in range(2):
        absorb(2 * qi + 1, 1, hh, (key <= qry)[:, :tk], upper)

    acc0, acc1 = acc_ref[0], acc_ref[1]
    o0 = acc0[:HEAD_DIM] / acc0[HEAD_DIM:HEAD_DIM + 1]
    o1 = acc1[FOX_ROWS - HEAD_DIM:] / acc1[0:1]
    n0 = o0 * lax.rsqrt(jnp.mean(o0 * o0, axis=0, keepdims=True) + NORM_EPS)
    n1 = o1 * lax.rsqrt(jnp.mean(o1 * o1, axis=0, keepdims=True) + NORM_EPS)
    o_ref[0] = (jnp.concatenate([n0, n1], axis=0).T * g_ref[...]).astype(_BF16)


def _fox_attn(qt, ka, vt, og, w_up, w_down, w_out, batch, seq):
    tk = FOX_KEYS
    tq = 2 * tk
    nq = seq // tq
    steps = batch * N_PAIRS * nq
    bf16_rows = 16
    assert seq % tq == 0 and all(w.shape[0] % (steps * bf16_rows) == 0 for w in (w_up, w_down, w_out))
    step = lambda b, p, i: ((b * N_PAIRS + p) * nq + i, 0)
    slab = lambda w: pl.BlockSpec((w.shape[0] // steps, w.shape[1]), step)
    return pl.pallas_call(
        functools.partial(_fox_attn_kernel, tq=tq, tk=tk),
        grid=(batch, N_PAIRS, nq),
        in_specs=[pl.BlockSpec((1, 2, FOX_ROWS, tq), lambda b, p, i: (b, p, 0, i)),
                  pl.BlockSpec((1, 2, seq, PAIR), lambda b, p, i: (b, p, 0, 0)),
                  pl.BlockSpec((1, 2, FOX_ROWS, seq), lambda b, p, i: (b, p, 0, 0)),
                  pl.BlockSpec((1, PAIR), lambda b, p, i: (0, p)),
                  slab(w_up), slab(w_down), slab(w_out)],
        out_specs=[pl.BlockSpec((1, tq, PAIR), lambda b, p, i: (b, i, p)), slab(w_up), slab(w_down), slab(w_out)],
        out_shape=[jax.ShapeDtypeStruct((batch, seq, D_HEADS), _BF16),
                   jax.ShapeDtypeStruct(w_up.shape, _BF16), jax.ShapeDtypeStruct(w_down.shape, _BF16),
                   jax.ShapeDtypeStruct(w_out.shape, _BF16)],
        scratch_shapes=[pltpu.VMEM((2, 1, tq), _F32), pltpu.VMEM((2, FOX_ROWS, tq), _F32),
                        pltpu.VMEM((2, 2, tk, tq), _F32)],
        compiler_params=_cparams(("parallel", "parallel", "arbitrary"), FOX_VMEM),
        name="fox_attn",
    )(qt, ka, vt, og, w_up, w_down, w_out)


def _stack_heads(x, lo_half):
    return jnp.concatenate([jnp.where(lo_half, x, 0.0), jnp.where(lo_half, 0.0, x)], axis=0)


def _interleave(*gens):
    live = list(gens)
    while live:
        for gen in list(live):
            try:
                next(gen)
            except StopIteration:
                live.remove(gen)


def _rwkv_kernel(r_ref, k_ref, v_ref, lo_ref, w0_ref, a0_ref, kk_ref, ka_ref, rk_ref, lg_ref, lb_ref,
                 db_ref, ab_ref, gb_ref, o_ref, z_ref, lhs_ref, add_ref, dec_ref, bv_ref, gate_ref,
                 *, tt, nt):
    nc = tt // CHUNK
    c2 = 2 * CHUNK
    t = pl.program_id(2)
    wr = lax.rem(t, 2)
    rd = 1 - wr
    lo_half = _lo_half_mask()

    @pl.when(t == 0)
    def _():
        z_ref[...] = jnp.zeros_like(z_ref)
        lhs_ref[:, 1] = jnp.zeros((PAIRS_PER_STEP,) + lhs_ref.shape[2:], lhs_ref.dtype)
        add_ref[:, 1] = jnp.zeros((PAIRS_PER_STEP,) + add_ref.shape[2:], add_ref.dtype)
        dec_ref[:, 1] = jnp.ones((PAIRS_PER_STEP,) + dec_ref.shape[2:], dec_ref.dtype)
        bv_ref[:, 1] = jnp.zeros((PAIRS_PER_STEP,) + bv_ref.shape[2:], bv_ref.dtype)
        gate_ref[:, 1] = jnp.zeros((PAIRS_PER_STEP,) + gate_ref.shape[2:], gate_ref.dtype)

    def advance(q):
        lanes = slice(q * PAIR, (q + 1) * PAIR)
        z = z_ref[q]
        dec = dec_ref[q, rd]
        ys = []
        for c in range(nc):
            res = jnp.dot(lhs_ref[q, rd, c], z.astype(_BF16), preferred_element_type=_F32)
            add = add_ref[q, rd, c]
            ysh = res[:c2] + add[:c2]
            ys.append(ysh[:CHUNK] + ysh[CHUNK:])
            z = dec[:, c:c + 1] * z + res[c2:] + add[c2:]
            yield
        z_ref[q] = z
        y = jnp.concatenate(ys, axis=0)
        mu = _half_sum(y, lo_half) * (1.0 / HEAD_DIM)
        dlt = y - mu
        var = _half_sum(dlt * dlt, lo_half) * (1.0 / HEAD_DIM)
        yn = dlt * lax.rsqrt(var + LNX_EPS) * lg_ref[:, lanes] + lb_ref[:, lanes]
        o_ref[0, :, lanes] = ((yn + bv_ref[q, rd]) * gate_ref[q, rd]).astype(_BF16)
        yield

    def lora():
        w = -_softplus(-(w0_ref[...] + jnp.dot(lo_ref[:, 0:PAIR], db_ref[...],
                                                 preferred_element_type=_F32))) - 0.5
        logd = -jnp.exp(w)
        a = _sigmoid(a0_ref[...] + jnp.dot(lo_ref[:, PAIR:2 * PAIR], ab_ref[...],
                                           preferred_element_type=_F32))
        gate = jnp.dot(lo_ref[:, 2 * PAIR:], gb_ref[...], preferred_element_type=_F32)
        return logd, a, gate

    def prepare(q, logd_all, a_all, gate_all):
        lanes = slice(q * PAIR, (q + 1) * PAIR)
        r = r_ref[:, lanes]
        k = k_ref[:, lanes]
        v = v_ref[:, lanes]
        logd = logd_all[:, lanes]
        a = a_all[:, lanes]
        gate_ref[q, wr] = gate_all[:, lanes]
        kk = k * kk_ref[:, lanes]
        kk = kk * lax.rsqrt(jnp.maximum(_half_sum(kk * kk, lo_half), 1e-24))
        k = k * (1.0 + (a - 1.0) * ka_ref[:, lanes])
        bb = kk * a
        bv_ref[q, wr] = _half_sum(r * k * rk_ref[:, lanes], lo_half) * v
        yield

        pos = lax.broadcasted_iota(jnp.int32, (tt, 1), 0) & (CHUNK - 1)
        lcum = logd
        for j in range(CHUNK_LOG2):
            lcum = lcum + jnp.where(pos >= (1 << j), pltpu.roll(lcum, 1 << j, axis=0), 0.0)
        yield

        r2 = lax.broadcasted_iota(jnp.int32, (c2, c2), 0)
        q2 = lax.broadcasted_iota(jnp.int32, (c2, c2), 1)
        same = (r2 >> CHUNK_LOG2) == (q2 >> CHUNK_LOG2)
        strict = same & (r2 > q2)
        incl = same & (r2 >= q2)
        eye = jnp.where(r2 == q2, 1.0, 0.0)
        zeros_cc = jnp.zeros((c2, PAIR), _F32)

        rng = range(nc)
        sl = [slice(c * CHUNK, (c + 1) * CHUNK) for c in rng]
        lc = [lcum[sl[c]] for c in rng]
        lc_rows = [lc[c][CHUNK - 1:CHUNK, :] for c in rng]
        dec_ref[q, wr] = jnp.exp(jnp.concatenate(lc_rows + [jnp.zeros((PAIR - nc, PAIR), _F32)], axis=0).T)
        e_pos = [jnp.exp(lc[c]) for c in rng]
        e_neg = [jnp.exp(-lc[c]) for c in rng]
        e_end = [jnp.exp(lc_rows[c] - lc[c]) for c in rng]
        rp = [_stack_heads(r[sl[c]] * e_pos[c], lo_half) for c in rng]
        ap = [_stack_heads(-kk[sl[c]] * jnp.exp(lc[c] - logd[sl[c]]), lo_half) for c in rng]
        kq = [_stack_heads(k[sl[c]] * e_neg[c], lo_half) for c in rng]
        bq = [_stack_heads(bb[sl[c]] * e_neg[c], lo_half) for c in rng]
        kc = [_stack_heads(k[sl[c]] * e_end[c], lo_half) for c in rng]
        bc = [_stack_heads(bb[sl[c]] * e_end[c], lo_half) for c in rng]
        vs = [_stack_heads(v[sl[c]], lo_half) for c in rng]
        yield

        s = [_dot_nt(jnp.concatenate([ap[c], rp[c]], axis=0), jnp.concatenate([bq[c], kq[c]], axis=0))
             for c in rng]
        a_ab = [jnp.where(strict, s[c][:c2, :c2], 0.0) for c in rng]
        a_ak = [jnp.where(strict, s[c][:c2, c2:], 0.0) for c in rng]
        a_rb = [jnp.where(incl, s[c][c2:, :c2], 0.0) for c in rng]
        a_rk = [jnp.where(incl, s[c][c2:, c2:], 0.0) for c in rng]
        yield
        akv = [_dot(a_ak[c], vs[c]) for c in rng]

        pw = [_dot(a_ab[c], a_ab[c]) for c in rng]
        tinv = [eye + a_ab[c] for c in rng]
        yield
        for _ in range(CHUNK_LOG2 - 2):
            pt = [_dot(pw[c], jnp.concatenate([pw[c], tinv[c]], axis=1)) for c in rng]
            pw = [pt[c][:, :c2] for c in rng]
            tinv = [tinv[c] + pt[c][:, c2:] for c in rng]
            yield
        q5 = [_dot(tinv[c], jnp.concatenate([ap[c], akv[c]], axis=1)) for c in rng]
        yield
        wu = [q5[c] + _dot(pw[c], q5[c]) for c in rng]
        yield
        x = [jnp.concatenate([wu[c], jnp.concatenate([zeros_cc, vs[c]], axis=1)], axis=0) for c in rng]
        ry = [_dot(jnp.concatenate([a_rb[c], a_rk[c]], axis=1), x[c]) for c in rng]
        mz = [_dot_tn(jnp.concatenate([bc[c], kc[c]], axis=0), x[c]) for c in rng]
        yield
        for c in rng:
            lhs_ref[q, wr, c] = jnp.concatenate([rp[c] + ry[c][:, :PAIR], mz[c][:, :PAIR]], axis=0).astype(_BF16)
            add_ref[q, wr, c] = jnp.concatenate([ry[c][:, PAIR:], mz[c][:, PAIR:]], axis=0)
        yield

    pairs = range(PAIRS_PER_STEP)

    @pl.when(t < nt)
    def _():
        shared = lora()
        _interleave(*[prepare(q, *shared) for q in pairs], *[advance(q) for q in pairs])

    @pl.when(t == nt)
    def _():
        _interleave(*[advance(q) for q in pairs])


def _rwkv(prw, lora_in, prm, batch, seq):
    tt = RWKV_ROWS
    nt = seq // tt
    nc = tt // CHUNK
    npp = PAIRS_PER_STEP
    width = npp * PAIR
    cur = lambda b, t: b * nt + jnp.minimum(t, nt - 1)
    blk = lambda c0: pl.BlockSpec((tt, width), lambda b, p, t, c0=c0: (cur(b, t), c0 + p))
    vec = pl.BlockSpec((1, width), lambda b, p, t: (0, p))
    lora = lambda rows: pl.BlockSpec((rows, width), lambda b, p, t: (0, p))
    per = D_HEADS // width
    return pl.pallas_call(
        functools.partial(_rwkv_kernel, tt=tt, nt=nt),
        grid=(batch, N_PAIRS // npp, nt + 1),
        in_specs=[blk(0), blk(per), blk(2 * per),
                  pl.BlockSpec((tt, SMALL_W), lambda b, p, t: (cur(b, t), 0)),
                  vec, vec, vec, vec, vec, vec, vec,
                  lora(PAIR), lora(PAIR), lora(GATE_LORA)],
        out_specs=pl.BlockSpec((1, tt, width), lambda b, p, t: (b, jnp.maximum(t - 1, 0), p)),
        out_shape=jax.ShapeDtypeStruct((batch, seq, D_HEADS), _BF16),
        scratch_shapes=[pltpu.VMEM((npp, PAIR, PAIR), _F32),
                        pltpu.VMEM((npp, 2, nc, 2 * PAIR, PAIR), _BF16),
                        pltpu.VMEM((npp, 2, nc, 2 * PAIR, PAIR), _F32),
                        pltpu.VMEM((npp, 2, PAIR, PAIR), _F32),
                        pltpu.VMEM((npp, 2, tt, PAIR), _F32),
                        pltpu.VMEM((npp, 2, tt, PAIR), _F32)],
        compiler_params=_cparams(("parallel", "parallel", "arbitrary"), RWKV_VMEM),
        name="rwkv",
    )(prw, prw, prw, lora_in, prm["w0"], prm["a0"], prm["k_k"], prm["k_a"], prm["r_k"],
      prm["lnx_g"], prm["lnx_b"], prm["decay_b"], prm["a_b"], prm["g_b"])


def _outproj_kernel(fox_ref, rw_ref, x_ref, w1_ref, w2_ref, gt_ref, g2_ref, sh_ref, sc_ref,
                    x1_ref, h2_ref):
    half = x_ref.shape[0] // 2
    for c in range(2):
        rows = slice(c * half, (c + 1) * half)
        y = (jnp.dot(fox_ref[rows, :], w1_ref[...], preferred_element_type=_F32)
             + jnp.dot(rw_ref[rows, :], w2_ref[...], preferred_element_type=_F32))
        x1 = x_ref[rows, :] + gt_ref[0] * y
        x1_ref[rows, :] = x1
        h2_ref[rows, :] = _adaln(x1, g2_ref, sh_ref, sc_ref)


def _outproj(fox, rw, x2, w_o, gt, g2, sh, sc, seq):
    m, d = x2.shape
    tm = OUTPROJ_ROWS
    per_b = seq // tm
    rows = lambda w: pl.BlockSpec((tm, w), lambda i: (i, 0))
    full = lambda a: pl.BlockSpec(a.shape, lambda i: (0, 0))
    mod = pl.BlockSpec((1, 1, d), lambda i: (i // per_b, 0, 0))
    return pl.pallas_call(
        _outproj_kernel,
        grid=(m // tm,),
        in_specs=[rows(D_HEADS), rows(D_HEADS), rows(d),
                  pl.BlockSpec((D_HEADS, d), lambda i: (0, 0)), pl.BlockSpec((D_HEADS, d), lambda i: (1, 0)),
                  mod, full(g2), mod, mod],
        out_specs=[rows(d), rows(d)],
        out_shape=[jax.ShapeDtypeStruct((m, d), _F32), jax.ShapeDtypeStruct((m, d), _BF16)],
        compiler_params=_cparams(("parallel",), OUTPROJ_VMEM),
        name="outproj",
    )(fox, rw, x2, w_o, w_o, gt, g2, sh, sc)


def _mlp_kernel(h_ref, wu_ref, wd_ref, x1_ref, gt_ref, o_ref):
    @pl.when(pl.program_id(1) == 0)
    def _():
        o_ref[...] = x1_ref[...]

    u = jnp.dot(h_ref[...], wu_ref[...], preferred_element_type=_F32)
    act = jnp.square(jnp.maximum(u, 0.0)).astype(_BF16)
    o_ref[...] += gt_ref[0] * jnp.dot(act, wd_ref[...], preferred_element_type=_F32)


def _mlp(h2, wu, wd, x1, gt, seq):
    m, d = x1.shape
    ff = wu.shape[1]
    tm, tf = MLP_ROWS, MLP_HIDDEN
    per_b = seq // tm
    rows = lambda **kw: pl.BlockSpec((tm, d), lambda i, f: (i, 0), **kw)
    return pl.pallas_call(
        _mlp_kernel,
        grid=(m // tm, ff // tf),
        in_specs=[rows(),
                  pl.BlockSpec((d, tf), lambda i, f: (0, f)),
                  pl.BlockSpec((tf, d), lambda i, f: (f, 0)),
                  rows(),
                  pl.BlockSpec((1, 1, d), lambda i, f: (i // per_b, 0, 0))],
        out_specs=rows(),
        out_shape=jax.ShapeDtypeStruct((m, d), _F32),
        compiler_params=_cparams(("parallel", "arbitrary"), MLP_VMEM),
        name="mlp",
    )(h2, wu, wd, x1, gt)


def _pad_rows(a, rows):
    return jnp.pad(a, ((0, rows - a.shape[0]), (0, 0)))


def _pad_lanes(a, width):
    return jnp.pad(a, ((0, 0), (0, width - a.shape[1])))


def _pack_kernel(wt_ref, o_ref):
    o_f = 3 * D_HEADS
    o_r = o_f + N_HEADS
    o_w = o_r + 3 * D_HEADS
    o_a = o_w + DECAY_LORA
    o_g = o_a + ICLR_LORA
    lanes = wt_ref.shape[1]
    dst = 0
    for src, size in ((0, o_f), (o_r, o_w - o_r), (o_w, DECAY_LORA), (o_f, N_HEADS),
                      (None, PAIR - DECAY_LORA - N_HEADS), (o_a, ICLR_LORA), (None, PAIR - ICLR_LORA),
                      (o_g, GATE_LORA)):
        if src is None:
            o_ref[dst:dst + size, :] = jnp.zeros((size, lanes), _BF16)
        else:
            o_ref[dst:dst + size, :] = wt_ref[src:src + size, :].astype(_BF16)
        dst += size


def _pack_w_in(w):
    wt = w.T
    n, d = wt.shape
    tl = PACK_LANES
    return pl.pallas_call(
        _pack_kernel,
        grid=(d // tl,),
        in_specs=[pl.BlockSpec((n, tl), lambda i: (0, i))],
        out_specs=pl.BlockSpec((6 * D_HEADS + SMALL_W, tl), lambda i: (0, i)),
        out_shape=jax.ShapeDtypeStruct((6 * D_HEADS + SMALL_W, d), _BF16),
        compiler_params=_cparams(("parallel",), PACK_VMEM),
        name="pack_w_in",
    )(wt)


def _pack_shift_mu(mu):
    o_w = 3 * D_HEADS
    o_a = o_w + DECAY_LORA
    o_g = o_a + ICLR_LORA
    return jnp.concatenate([jnp.zeros((1, 3 * D_HEADS), _F32), mu[:, :o_w], _pad_lanes(mu[:, o_w:o_a], PAIR),
                            _pad_lanes(mu[:, o_a:o_g], PAIR), mu[:, o_g:]], axis=1)


def kernel(x, c, w_ada, b_ada, norm1_g, w_in, b_forget, q_norm_g, k_norm_g, fox_out_g, shift_mu, w0, decay_b, a0, a_b, g_b, k_k, k_a, r_k, lnx_g, lnx_b, w_out, norm2_g, w_mlp_up, w_mlp_down):
    batch, seq, d = x.shape
    depth = w_ada.shape[0]
    assert w_in.shape[2] == 6 * D_HEADS + N_HEADS + DECAY_LORA + ICLR_LORA + GATE_LORA
    assert seq % max(PROJ_ROWS, RWKV_ROWS, FOX_PREP_ROWS, OUTPROJ_ROWS, MLP_ROWS) == 0 and d % PACK_LANES == 0
    assert w_mlp_up.shape[2] % MLP_HIDDEN == 0 and (6 * d) % ADA_COLS == 0
    x2 = x.reshape(batch * seq, d)
    row = lambda a: a.reshape(1, -1)
    for l in range(depth):
        mod = _ada(c, w_ada[l], b_ada[l])
        sh1, sc1, gt1, sh2, sc2, gt2 = [m.reshape(batch, 1, d) for m in jnp.split(mod, 6, axis=-1)]

        pfox, prw = _proj(x2, row(norm1_g[l]), sh1, sc1, _pack_w_in(w_in[l]), _pack_shift_mu(row(shift_mu[l])), seq)

        bf_slot = jnp.pad(row(b_forget[l]), ((0, 0), (F_LANE0, PAIR - F_LANE0 - N_HEADS)))
        qt, ka, vt, lora_in = _fox_prep(pfox, prw, jnp.tile(q_norm_g[l], 2).reshape(PAIR, 1),
                                        jnp.tile(k_norm_g[l], 2).reshape(PAIR, 1), bf_slot, batch, seq)
        fox, w_up16, w_down16, w_out16 = _fox_attn(qt, ka, vt, row(fox_out_g[l]), w_mlp_up[l], w_mlp_down[l],
                                                   w_out[l], batch, seq)

        prm = {
            "w0": row(w0[l]), "a0": row(a0[l]), "k_k": row(k_k[l]), "k_a": row(k_a[l]),
            "r_k": row(r_k[l]), "lnx_g": row(lnx_g[l]), "lnx_b": row(lnx_b[l]),
            "decay_b": _pad_rows(decay_b[l], PAIR).astype(_BF16),
            "a_b": _pad_rows(a_b[l], PAIR).astype(_BF16),
            "g_b": g_b[l].astype(_BF16),
        }
        rw = _rwkv(prw, lora_in, prm, batch, seq)

        x1, h2 = _outproj(fox.reshape(batch * seq, D_HEADS), rw.reshape(batch * seq, D_HEADS), x2,
                          w_out16, gt1, row(norm2_g[l]), sh2, sc2, seq)
        x2 = _mlp(h2, w_up16, w_down16, x1, gt2, seq)
    return x2.reshape(batch, seq, d)
```
